```python
import math
import jax
import jax.numpy as jnp
from jax import lax
import numpy as np

D_MODEL = 4096
BATCH = 32
SEQ = 256
DEPTH = 2
DEC_BATCH = 4
DEC_SEQ = 2048
PAST_LEN = 512

GRID_W = 64
HEAD_DIM = 128
ATTN_WIDTH = D_MODEL // 4
ATTN_HEADS = ATTN_WIDTH // HEAD_DIM
KV_HEADS = ATTN_HEADS // 4
KV_WIDTH = KV_HEADS * HEAD_DIM
CONV_WIDTH = D_MODEL // 4
CONV_KERNEL = 31
POOL_WIDTH = D_MODEL // 4
POOL_WINDOWS = (2, 4, 8, 16)
POOL_GROUP = POOL_WIDTH // len(POOL_WINDOWS)
SCONV_WIDTH = D_MODEL // 4
SCONV_KERNEL = 3
MIX_WIDTH = ATTN_WIDTH + CONV_WIDTH + POOL_WIDTH + SCONV_WIDTH
IN_SPLITS = (
    ATTN_WIDTH,
    ATTN_WIDTH + KV_WIDTH,
    ATTN_WIDTH + 2 * KV_WIDTH,
    ATTN_WIDTH + 2 * KV_WIDTH + 2 * CONV_WIDTH,
    ATTN_WIDTH + 2 * KV_WIDTH + 2 * CONV_WIDTH + POOL_WIDTH,
)
IN_WIDTH = ATTN_WIDTH + 2 * KV_WIDTH + 2 * CONV_WIDTH + POOL_WIDTH + 3 * SCONV_WIDTH
D_FF = 11008
Q_BLOCK = 128
ROPE_THETA = 10000.0
N_MOD = 9
ALPHA = float((2 * DEPTH) ** 0.25)
BETA = float((8 * DEPTH) ** -0.25)
EPS = 1e-6

kernel_name = "hybrid_flow_parallel_groups_step"


def layer_norm(x, g=None, b=None):
    xf = x.astype(jnp.float32)
    mu = jnp.mean(xf, axis=-1, keepdims=True)
    var = jnp.mean(jnp.square(xf - mu), axis=-1, keepdims=True)
    y = (xf - mu) * lax.rsqrt(var + EPS)
    if g is not None:
        y = y * g.astype(jnp.float32) + b.astype(jnp.float32)
    return y.astype(x.dtype)


def rms_norm(x, g):
    xf = x.astype(jnp.float32)
    y = xf * lax.rsqrt(jnp.mean(jnp.square(xf), axis=-1, keepdims=True) + EPS) * g.astype(jnp.float32)
    return y.astype(x.dtype)


def axial_rope(x):
    T = x.shape[2]
    rows = T // GRID_W
    row = jnp.repeat(jnp.arange(rows, dtype=jnp.float32), GRID_W)
    col = jnp.tile(jnp.arange(GRID_W, dtype=jnp.float32), rows)
    half = HEAD_DIM // 2
    inv = ROPE_THETA ** (-jnp.arange(0, half, 2, dtype=jnp.float32) / half)
    xf = x.astype(jnp.float32)

    def rot(xa, pos):
        ang = pos[:, None] * inv[None, :]
        cos, sin = jnp.cos(ang), jnp.sin(ang)
        x1, x2 = xa[..., : half // 2], xa[..., half // 2:]
        return jnp.concatenate([x1 * cos - x2 * sin, x2 * cos + x1 * sin], axis=-1)

    out = jnp.concatenate([rot(xf[..., :half], row), rot(xf[..., half:], col)], axis=-1)
    return out.astype(x.dtype)


def block_attention(q, k, v):
    B, H, T, Dh = q.shape
    G = H // KV_HEADS
    nb = T // Q_BLOCK
    scale = 1.0 / math.sqrt(Dh)
    qb = q.reshape(B, KV_HEADS, G, nb, Q_BLOCK, Dh).transpose(3, 0, 1, 2, 4, 5)
    kf = k.astype(jnp.float32)
    vf = v.astype(jnp.float32)

    def one_block(qblk):
        s = jnp.einsum('bkgqd,bksd->bkgqs', qblk.astype(jnp.float32), kf) * scale
        p = jax.nn.softmax(s, axis=-1)
        return jnp.einsum('bkgqs,bksd->bkgqd', p, vf).astype(q.dtype)

    o = lax.map(one_block, qb)
    return o.transpose(1, 0, 4, 2, 3, 5).reshape(B, T, H * Dh)


def depthwise_conv(x, w):
    return lax.conv_general_dilated(
        x, w[:, None, :].astype(x.dtype), window_strides=(1,), padding='SAME',
        dimension_numbers=('NWC', 'WIO', 'NWC'), feature_group_count=x.shape[-1])


def multiscale_pool(x):
    B, T, C = x.shape
    xf = x.astype(jnp.float32)
    cs = jnp.concatenate([jnp.zeros((B, 1, C), jnp.float32), jnp.cumsum(xf, axis=1)], axis=1)
    t = jnp.arange(T)
    outs = []
    for gi, w in enumerate(POOL_WINDOWS):
        sl = cs[..., gi * POOL_GROUP:(gi + 1) * POOL_GROUP]
        lo = jnp.clip(t - w // 2, 0, T)
        hi = jnp.clip(t + w // 2, 0, T)
        cnt = (hi - lo).astype(jnp.float32)
        outs.append((sl[:, hi] - sl[:, lo]) / cnt[None, :, None])
    return (jnp.concatenate(outs, axis=-1) - xf).astype(x.dtype)


def swiglu(u, w_gu, w_down):
    g, up = jnp.split(u @ w_gu, 2, axis=-1)
    return (jax.nn.silu(g) * up) @ w_down


def token_mixer(u, l, P, ctx_k, ctx_v):
    B, T, _ = u.shape
    h = u @ P['w_in'][l]
    q, k, v, a, pl, sc = jnp.split(h, IN_SPLITS, axis=-1)
    q = rms_norm(q.reshape(B, T, ATTN_HEADS, HEAD_DIM).transpose(0, 2, 1, 3), P['q_norm_g'][l])
    k = rms_norm(k.reshape(B, T, KV_HEADS, HEAD_DIM).transpose(0, 2, 1, 3), P['k_norm_g'][l])
    v = v.reshape(B, T, KV_HEADS, HEAD_DIM).transpose(0, 2, 1, 3)
    if ctx_k is None:
        attn = block_attention(q, k, v)
    else:
        q = axial_rope(q)
        k_lat = axial_rope(k)
        k_all = jnp.concatenate([k_lat, ctx_k.astype(k_lat.dtype)], axis=2)
        v_all = jnp.concatenate([v, ctx_v.astype(v.dtype)], axis=2)
        attn = block_attention(q, k_all, v_all)
    a_val, a_gate = jnp.split(a, 2, axis=-1)
    a = a_val * jax.nn.sigmoid(a_gate)
    a = depthwise_conv(a, P['conv_dw_w'][l]) + P['conv_dw_b'][l]
    a = jax.nn.silu(layer_norm(a, P['conv_ln_g'][l], P['conv_ln_b'][l]))
    pooled = multiscale_pool(pl).reshape(B, T, len(POOL_WINDOWS), POOL_GROUP)
    pool_out = jnp.einsum('btgc,gcd->btgd', pooled, P['pool_w'][l]).reshape(B, T, POOL_WIDTH) * P['pool_scale'][l]
    bg, cg, xs = jnp.split(sc, 3, axis=-1)
    s = bg * depthwise_conv(cg * xs, P['sconv_w'][l])
    out = jnp.concatenate([attn, a, pool_out, s], axis=-1) @ P['w_out'][l]
    return out, k, v


def modulation(cvec, l, P):
    m = jax.nn.silu(cvec) @ P['w_mod'][l] + P['b_mod'][l]
    return m.reshape(cvec.shape[0], N_MOD, D_MODEL)


def trunk_layer(x, mod, l, P, ctx_k, ctx_v):
    mod = mod.astype(x.dtype)

    def modulate(y, i):
        return layer_norm(y) * (1 + mod[:, 3 * i + 1, None, :]) + mod[:, 3 * i, None, :]

    def gate(i):
        return mod[:, 3 * i + 2, None, :]

    f1 = swiglu(modulate(x, 0), P['ffn_w_gu'][l, 0], P['ffn_w_down'][l, 0])
    x = layer_norm(ALPHA * x + 0.5 * gate(0) * f1, P['ln_g'][l, 0], P['ln_b'][l, 0])
    mix, k, v = token_mixer(modulate(x, 1), l, P, ctx_k, ctx_v)
    x = layer_norm(ALPHA * x + gate(1) * mix, P['ln_g'][l, 1], P['ln_b'][l, 1])
    f2 = swiglu(modulate(x, 2), P['ffn_w_gu'][l, 1], P['ffn_w_down'][l, 1])
    x = layer_norm(ALPHA * x + 0.5 * gate(2) * f2, P['ln_g'][l, 2], P['ln_b'][l, 2])
    return x, k, v


def setup_inputs(seed: int = 0) -> dict:
    key = jax.random.key(seed)
    ks = jax.random.split(key, 24)
    f32 = jnp.float32
    nrm = lambda k, shape, s: jax.random.normal(k, shape, f32) * s
    kv_shape = (DEC_BATCH, DEPTH, KV_HEADS, PAST_LEN, HEAD_DIM)
    return {
        'x_prompt': nrm(ks[0], (BATCH, SEQ, D_MODEL), 1.0),
        'x_sample': nrm(ks[1], (DEC_BATCH, DEC_SEQ, D_MODEL), 1.0),
        'cache_k': nrm(ks[2], kv_shape, 1.0),
        'cache_v': nrm(ks[3], kv_shape, 1.0),
        'c': nrm(ks[4], (DEC_BATCH, D_MODEL), 1.0),
        'c_ctx': nrm(ks[5], (D_MODEL,), 1.0),
        'w_mod': nrm(ks[6], (DEPTH, D_MODEL, N_MOD * D_MODEL), 0.5 * D_MODEL ** -0.5),
        'b_mod': nrm(ks[7], (DEPTH, N_MOD * D_MODEL), 0.02),
        'ln_g': 1.0 + nrm(ks[8], (DEPTH, 3, D_MODEL), 0.02),
        'ln_b': nrm(ks[9], (DEPTH, 3, D_MODEL), 0.02),
        'ffn_w_gu': nrm(ks[10], (DEPTH, 2, D_MODEL, 2 * D_FF), D_MODEL ** -0.5),
        'ffn_w_down': nrm(ks[11], (DEPTH, 2, D_FF, D_MODEL), BETA * D_FF ** -0.5),
        'w_in': nrm(ks[12], (DEPTH, D_MODEL, IN_WIDTH), D_MODEL ** -0.5),
        'q_norm_g': 1.0 + nrm(ks[13], (DEPTH, HEAD_DIM), 0.02),
        'k_norm_g': 1.0 + nrm(ks[14], (DEPTH, HEAD_DIM), 0.02),
        'conv_dw_w': nrm(ks[15], (DEPTH, CONV_KERNEL, CONV_WIDTH), CONV_KERNEL ** -0.5),
        'conv_dw_b': nrm(ks[16], (DEPTH, CONV_WIDTH), 0.02),
        'conv_ln_g': 1.0 + nrm(ks[17], (DEPTH, CONV_WIDTH), 0.02),
        'conv_ln_b': nrm(ks[18], (DEPTH, CONV_WIDTH), 0.02),
        'pool_w': nrm(ks[19], (DEPTH, len(POOL_WINDOWS), POOL_GROUP, POOL_GROUP), POOL_GROUP ** -0.5),
        'pool_scale': 1.0 + nrm(ks[20], (DEPTH, POOL_WIDTH), 0.02),
        'sconv_w': nrm(ks[21], (DEPTH, SCONV_KERNEL, SCONV_WIDTH), SCONV_KERNEL ** -0.5),
        'w_out': nrm(ks[22], (DEPTH, MIX_WIDTH, D_MODEL), BETA * MIX_WIDTH ** -0.5),
    }


def reference(x_prompt, x_sample, cache_k, cache_v, c, c_ctx, w_mod, b_mod, ln_g, ln_b,
              ffn_w_gu, ffn_w_down, w_in, q_norm_g, k_norm_g, conv_dw_w, conv_dw_b,
              conv_ln_g, conv_ln_b, pool_w, pool_scale, sconv_w, w_out):
    P = {
        'w_mod': w_mod, 'b_mod': b_mod, 'ln_g': ln_g, 'ln_b': ln_b,
        'ffn_w_gu': ffn_w_gu, 'ffn_w_down': ffn_w_down, 'w_in': w_in,
        'q_norm_g': q_norm_g, 'k_norm_g': k_norm_g, 'conv_dw_w': conv_dw_w,
        'conv_dw_b': conv_dw_b, 'conv_ln_g': conv_ln_g, 'conv_ln_b': conv_ln_b,
        'pool_w': pool_w, 'pool_scale': pool_scale, 'sconv_w': sconv_w, 'w_out': w_out,
    }
    xp = x_prompt
    k_list, v_list = [], []
    for l in range(DEPTH):
        xp, k_l, v_l = trunk_layer(xp, modulation(c_ctx[None, :], l, P), l, P, None, None)
        k_list.append(k_l)
        v_list.append(v_l)
    state_k = jnp.stack(k_list, axis=1)
    state_v = jnp.stack(v_list, axis=1)
    xs = x_sample
    for l in range(DEPTH):
        xs, _, _ = trunk_layer(xs, modulation(c, l, P), l, P, cache_k[:, l], cache_v[:, l])
    return (xp, xs, state_k, state_v)
```

```python
import functools
import math

import jax
import jax.numpy as jnp
from jax import lax
from jax.experimental import pallas as pl
from jax.experimental.pallas import tpu as pltpu

GRID_W = 64
HEAD_DIM = 128
KV_HEADS = 2
Q_PER_KV = 4
CONV_KERNEL = 31
POOL_WINDOWS = (2, 4, 8, 16)
SCONV_KERNEL = 3
ROPE_THETA = 10000.0
N_MOD = 9
EPS = 1e-6

V7X_VMEM_BYTES = 64 * 1024 * 1024
V7X_SUBLANES = 8
V7X_LANES = 128

BF16 = jnp.bfloat16
F32 = jnp.float32


def _nbytes(shape, dtype):
    return math.prod(shape) * jnp.dtype(dtype).itemsize


def _vmem_limit(double_buffered, single_buffered=()):
    need = 2 * sum(_nbytes(s, d) for s, d in double_buffered)
    need += sum(_nbytes(s, d) for s, d in single_buffered)
    need += need // 4 + (4 << 20)
    return int(min(need, V7X_VMEM_BYTES - (4 << 20)))


def _params(semantics, vmem):
    return pltpu.CompilerParams(dimension_semantics=semantics, vmem_limit_bytes=vmem)


def _normalize(x):
    mu = jnp.mean(x, axis=-1, keepdims=True)
    xc = x - mu
    var = jnp.mean(xc * xc, axis=-1, keepdims=True)
    return xc * lax.rsqrt(var + EPS)


def _silu(x):
    return x * jax.nn.sigmoid(x)


ROW_CHUNK = 2 * V7X_SUBLANES


def _row_chunks(n_rows, fn):
    def body(c, carry):
        fn(pl.ds(pl.multiple_of(c * ROW_CHUNK, ROW_CHUNK), ROW_CHUNK))
        return carry
    lax.fori_loop(0, n_rows // ROW_CHUNK, body, 0)


def _modulate_into(x_ref, mod_ref, sub, xn_ref):
    shift = mod_ref[pl.ds(3 * sub, 1), :]
    scale1 = 1.0 + mod_ref[pl.ds(3 * sub + 1, 1), :]

    def chunk(rows):
        xn_ref[rows, :] = (_normalize(x_ref[rows, :]) * scale1 + shift).astype(xn_ref.dtype)
    _row_chunks(x_ref.shape[0], chunk)


def _residual_norm_into(x_ref, mod_ref, sub, branch_scale, alpha, lng_ref, lnb_ref, o_ref):
    gate = branch_scale * mod_ref[pl.ds(3 * sub + 2, 1), :]
    g = lng_ref[...]
    b = lnb_ref[...]

    def chunk(rows):
        z = alpha * x_ref[rows, :] + gate * o_ref[rows, :]
        o_ref[rows, :] = _normalize(z) * g + b
    _row_chunks(x_ref.shape[0], chunk)


def _mod_body(c_ref, w_ref, b_ref, o_ref):
    s = _silu(c_ref[...]).astype(BF16)
    o_ref[...] = jnp.dot(s, w_ref[...].astype(BF16), preferred_element_type=F32) + b_ref[...]


def _modulation(cvec, w_mod, b_mod, *, tn=512):
    depth, d, n = w_mod.shape
    r = cvec.shape[0]
    return pl.pallas_call(
        _mod_body,
        grid=(depth, n // tn),
        in_specs=[
            pl.BlockSpec((r, d), lambda l, j: (0, 0)),
            pl.BlockSpec((None, d, tn), lambda l, j: (l, 0, j)),
            pl.BlockSpec((None, 1, tn), lambda l, j: (l, 0, j)),
        ],
        out_specs=pl.BlockSpec((None, r, tn), lambda l, j: (l, 0, j)),
        out_shape=jax.ShapeDtypeStruct((depth, r, n), F32),
        compiler_params=_params(
            ("arbitrary", "arbitrary"),
            _vmem_limit([((d, tn), F32), ((r, d), F32), ((r, tn), F32)], [((d, tn), BF16)])),
        name="modulation",
    )(cvec, w_mod, b_mod.reshape(depth, 1, n))


def _ffn_body(x_ref, mod_ref, wg_ref, wu_ref, wd_ref, lng_ref, lnb_ref, o_ref, xn_ref, *, sub, alpha):
    f = pl.program_id(1)

    @pl.when(f == 0)
    def _():
        _modulate_into(x_ref, mod_ref, sub, xn_ref)
        o_ref[...] = jnp.zeros_like(o_ref)

    xn = xn_ref[...]
    g = jnp.dot(xn, wg_ref[...], preferred_element_type=F32)
    u = jnp.dot(xn, wu_ref[...], preferred_element_type=F32)
    act = (_silu(g) * u).astype(BF16)
    o_ref[...] += jnp.dot(act, wd_ref[...], preferred_element_type=F32)

    @pl.when(f == pl.num_programs(1) - 1)
    def _():
        _residual_norm_into(x_ref, mod_ref, sub, 0.5, alpha, lng_ref, lnb_ref, o_ref)


def _mod_index(n_mod_rows, tiles_per_batch):
    if n_mod_rows == 1:
        return lambda i, j: (0, 0, 0)
    return lambda i, j: (i // tiles_per_batch, 0, 0)


def _ffn(x, mod, w_gu, w_down, ln_g, ln_b, *, layer, ffn_idx, sub, alpha, seq, tm=512, tf=256):
    m, d = x.shape
    ff = w_down.shape[2]
    nf = ff // tf
    assert m % tm == 0 and ff % tf == 0 and (seq % tm == 0 or mod.shape[0] == 1)
    body = functools.partial(_ffn_body, sub=sub, alpha=alpha)
    return pl.pallas_call(
        body,
        grid=(m // tm, nf),
        in_specs=[
            pl.BlockSpec((tm, d), lambda i, f: (i, 0)),
            pl.BlockSpec((None, N_MOD, d), _mod_index(mod.shape[0], max(seq // tm, 1))),
            pl.BlockSpec((None, None, d, tf), lambda i, f: (layer, ffn_idx, 0, f)),
            pl.BlockSpec((None, None, d, tf), lambda i, f: (layer, ffn_idx, 0, nf + f)),
            pl.BlockSpec((None, None, tf, d), lambda i, f: (layer, ffn_idx, f, 0)),
            pl.BlockSpec((None, 1, d), lambda i, f: (layer * 3 + sub, 0, 0)),
            pl.BlockSpec((None, 1, d), lambda i, f: (layer * 3 + sub, 0, 0)),
        ],
        out_specs=pl.BlockSpec((tm, d), lambda i, f: (i, 0)),
        out_shape=jax.ShapeDtypeStruct((m, d), F32),
        scratch_shapes=[pltpu.VMEM((tm, d), BF16)],
        compiler_params=_params(
            ("parallel", "arbitrary"),
            _vmem_limit([((tm, d), F32), ((tm, d), F32), ((d, 2 * tf), BF16), ((tf, d), BF16),
                         ((N_MOD, d), F32)],
                        [((tm, d), BF16), ((tm, 3 * tf), F32)])),
        name="ffn",
    )(x, mod, w_gu, w_gu, w_down, ln_g, ln_b)


def _in_proj_body(x_ref, mod_ref, w_ref, o_ref, xn_ref):
    @pl.when(pl.program_id(1) == 0)
    def _():
        _modulate_into(x_ref, mod_ref, 1, xn_ref)

    o_ref[...] = jnp.dot(xn_ref[...], w_ref[...], preferred_element_type=F32)


def _in_proj(x, mod, w_in, *, layer, seq, tm=512, tn=512):
    m, d = x.shape
    n = w_in.shape[2]
    assert m % tm == 0 and n % tn == 0 and (seq % tm == 0 or mod.shape[0] == 1)
    return pl.pallas_call(
        _in_proj_body,
        grid=(m // tm, n // tn),
        in_specs=[
            pl.BlockSpec((tm, d), lambda i, j: (i, 0)),
            pl.BlockSpec((None, N_MOD, d), _mod_index(mod.shape[0], max(seq // tm, 1))),
            pl.BlockSpec((None, d, tn), lambda i, j: (layer, 0, j)),
        ],
        out_specs=pl.BlockSpec((tm, tn), lambda i, j: (i, j)),
        out_shape=jax.ShapeDtypeStruct((m, n), F32),
        scratch_shapes=[pltpu.VMEM((tm, d), BF16)],
        compiler_params=_params(
            ("parallel", "arbitrary"),
            _vmem_limit([((tm, d), F32), ((d, tn), BF16), ((tm, tn), F32), ((N_MOD, d), F32)],
                        [((tm, d), BF16)])),
        name="in_proj",
    )(x, mod, w_in)


def _rms(x, g):
    return x * lax.rsqrt(jnp.mean(x * x, axis=-1, keepdims=True) + EPS) * g


def _rope(x, cos, sin_signed):
    q = HEAD_DIM // 4
    lane = lax.broadcasted_iota(jnp.int32, x.shape, 1)
    partner = jnp.where((lane // q) % 2 == 0,
                        pltpu.roll(x, HEAD_DIM - q, axis=1),
                        pltpu.roll(x, q, axis=1))
    return x * cos + partner * sin_signed


def _attn_body(*refs, rope, ctx, emit_state, seq):
    it = iter(refs)
    q_ref, k_ref, v_ref = next(it), next(it), next(it)
    gq_ref, gk_ref = next(it), next(it)
    if rope:
        cosq_ref, sinq_ref, cosk_ref, sink_ref = next(it), next(it), next(it), next(it)
    if ctx:
        ck_ref, cv_ref = next(it), next(it)
    o_ref = next(it)
    if emit_state:
        ko_ref, vo_ref = next(it), next(it)
    kall_ref, vall_ref = next(it), next(it)

    @pl.when(pl.program_id(2) == 0)
    def _():
        kn = _rms(k_ref[...], gk_ref[...])
        v = v_ref[...]
        if emit_state:
            ko_ref[...] = kn
            vo_ref[...] = v
        if rope:
            kn = _rope(kn, cosk_ref[...], sink_ref[...])
        kall_ref[pl.ds(0, seq), :] = kn.astype(BF16)
        vall_ref[pl.ds(0, seq), :] = v.astype(BF16)
        if ctx:
            past = ck_ref.shape[0]
            kall_ref[pl.ds(seq, past), :] = ck_ref[...].astype(BF16)
            vall_ref[pl.ds(seq, past), :] = cv_ref[...].astype(BF16)

    scale = 1.0 / math.sqrt(HEAD_DIM)
    k_all = kall_ref[...]
    v_all = vall_ref[...]
    for h in range(Q_PER_KV):
        cols = pl.ds(h * HEAD_DIM, HEAD_DIM)
        qh = _rms(q_ref[:, cols], gq_ref[...])
        if rope:
            qh = _rope(qh, cosq_ref[...], sinq_ref[...])
        s = lax.dot_general((qh * scale).astype(BF16), k_all, (((1,), (1,)), ((), ())),
                            preferred_element_type=F32)
        p = jnp.exp(s - jnp.max(s, axis=-1, keepdims=True))
        denom = jnp.sum(p, axis=-1, keepdims=True)
        o = jnp.dot(p.astype(BF16), v_all, preferred_element_type=F32) / denom
        o_ref[:, cols] = o.astype(o_ref.dtype)


def _rope_tables(seq):
    rows = seq // GRID_W
    row = jnp.repeat(jnp.arange(rows, dtype=F32), GRID_W)
    col = jnp.tile(jnp.arange(GRID_W, dtype=F32), rows)
    half = HEAD_DIM // 2
    inv = ROPE_THETA ** (-jnp.arange(0, half, 2, dtype=F32) / half)
    ar = row[:, None] * inv[None, :]
    ac = col[:, None] * inv[None, :]
    cos = jnp.concatenate([jnp.cos(ar), jnp.cos(ar), jnp.cos(ac), jnp.cos(ac)], axis=-1)
    sin = jnp.concatenate([-jnp.sin(ar), jnp.sin(ar), -jnp.sin(ac), jnp.sin(ac)], axis=-1)
    return cos, sin


def _attention(h, q_norm_g, k_norm_g, *, layer, batch, seq, ctx_k=None, ctx_v=None, emit_state, tq=256):
    rope = ctx_k is not None
    ctx = ctx_k is not None
    tq = min(tq, seq)
    nq = seq // tq
    group_w = Q_PER_KV * HEAD_DIM
    k_col0 = KV_HEADS * Q_PER_KV
    v_col0 = k_col0 + KV_HEADS
    past = ctx_k.shape[3] if ctx else 0

    in_specs = [
        pl.BlockSpec((tq, group_w), lambda b, g, i: (b * nq + i, g)),
        pl.BlockSpec((seq, HEAD_DIM), lambda b, g, i: (b, k_col0 + g)),
        pl.BlockSpec((seq, HEAD_DIM), lambda b, g, i: (b, v_col0 + g)),
        pl.BlockSpec((None, 1, HEAD_DIM), lambda b, g, i: (layer, 0, 0)),
        pl.BlockSpec((None, 1, HEAD_DIM), lambda b, g, i: (layer, 0, 0)),
    ]
    args = [h, h, h, q_norm_g, k_norm_g]
    blocks = [((tq, group_w), F32), ((seq, HEAD_DIM), F32), ((seq, HEAD_DIM), F32)]
    if rope:
        cos, sin = _rope_tables(seq)
        in_specs += [
            pl.BlockSpec((tq, HEAD_DIM), lambda b, g, i: (i, 0)),
            pl.BlockSpec((tq, HEAD_DIM), lambda b, g, i: (i, 0)),
            pl.BlockSpec((seq, HEAD_DIM), lambda b, g, i: (0, 0)),
            pl.BlockSpec((seq, HEAD_DIM), lambda b, g, i: (0, 0)),
        ]
        args += [cos, sin, cos, sin]
        blocks += [((seq, HEAD_DIM), F32)] * 2
    if ctx:
        in_specs += [pl.BlockSpec((None, None, None, past, HEAD_DIM), lambda b, g, i: (b, layer, g, 0, 0))] * 2
        args += [ctx_k, ctx_v]
        blocks += [((past, HEAD_DIM), F32)] * 2

    out_specs = [pl.BlockSpec((tq, group_w), lambda b, g, i: (b * nq + i, g))]
    out_shape = [jax.ShapeDtypeStruct((batch * seq, KV_HEADS * group_w), BF16)]
    if emit_state:
        out_specs += [pl.BlockSpec((None, None, seq, HEAD_DIM), lambda b, g, i: (b, g, 0, 0))] * 2
        out_shape += [jax.ShapeDtypeStruct((batch, KV_HEADS, seq, HEAD_DIM), F32)] * 2
        blocks += [((seq, HEAD_DIM), F32)] * 2
    total = seq + past
    body = functools.partial(_attn_body, rope=rope, ctx=ctx, emit_state=emit_state, seq=seq)
    return pl.pallas_call(
        body,
        grid=(batch, KV_HEADS, nq),
        in_specs=in_specs,
        out_specs=out_specs,
        out_shape=out_shape,
        scratch_shapes=[pltpu.VMEM((total, HEAD_DIM), BF16), pltpu.VMEM((total, HEAD_DIM), BF16)],
        compiler_params=_params(
            ("parallel", "parallel", "arbitrary"),
            _vmem_limit(blocks, [((total, HEAD_DIM), BF16)] * 2 + [((tq, total), F32)] * 3)),
        name="attention",
    )(*args)


_PAD = 16
TAP_ROWS = 32


def _tap_loop(pad_ref, seq, offsets, taps_ref, emit):
    window = TAP_ROWS + 2 * _PAD
    shifts = [_PAD + off for off in offsets]

    def body(c, carry):
        r = pl.multiple_of(c * TAP_ROWS, TAP_ROWS)
        win = pad_ref[pl.ds(r, window), :]
        acc = None
        for phase in sorted({s % V7X_SUBLANES for s in shifts}):
            rolled = win if phase == 0 else pltpu.roll(win, window - phase, axis=0)
            for j, s in enumerate(shifts):
                if s % V7X_SUBLANES != phase:
                    continue
                term = rolled[s - phase:s - phase + TAP_ROWS, :]
                if taps_ref is not None:
                    term = term * taps_ref[pl.ds(j, 1), :]
                acc = term if acc is None else acc + term
        emit(pl.ds(r, TAP_ROWS), r, acc)
        return carry
    lax.fori_loop(0, seq // TAP_ROWS, body, 0)


def _fill_loop(pad_ref, seq, fn):
    def body(c, carry):
        r = pl.multiple_of(c * TAP_ROWS, TAP_ROWS)
        pad_ref[pl.ds(r + _PAD, TAP_ROWS), :] = fn(pl.ds(r, TAP_ROWS))
        return carry
    lax.fori_loop(0, seq // TAP_ROWS, body, 0)


def _local_body(av_ref, ag_ref, pool_ref, bg_ref, cg_ref, xs_ref, cw_ref, cb_ref, pw_ref, ps_ref, sw_ref,
                conv_ref, poolo_ref, s_ref, pad_ref, pooled_ref, *, seq):
    zeros = jnp.zeros((_PAD, pad_ref.shape[1]), F32)
    pad_ref[pl.ds(0, _PAD), :] = zeros
    pad_ref[pl.ds(_PAD + seq, _PAD), :] = zeros

    _fill_loop(pad_ref, seq, lambda rows: av_ref[rows, :] * jax.nn.sigmoid(ag_ref[rows, :]))
    half = CONV_KERNEL // 2

    def emit_conv(rows, r, acc):
        conv_ref[rows, :] = acc + cb_ref[...]
    _tap_loop(pad_ref, seq, range(-half, half + 1), cw_ref, emit_conv)

    _fill_loop(pad_ref, seq, lambda rows: cg_ref[rows, :] * xs_ref[rows, :])
    half = SCONV_KERNEL // 2

    def emit_sconv(rows, r, acc):
        s_ref[rows, :] = (bg_ref[rows, :] * acc).astype(s_ref.dtype)
    _tap_loop(pad_ref, seq, range(-half, half + 1), sw_ref, emit_sconv)

    _fill_loop(pad_ref, seq, lambda rows: pool_ref[rows, :])
    group = pl.program_id(1)
    for gi, w in enumerate(POOL_WINDOWS):
        @pl.when(group == gi)
        def _(w=w):
            def emit_pool(rows, r, acc):
                t = r + lax.broadcasted_iota(jnp.int32, (TAP_ROWS, 1), 0)
                cnt = (jnp.minimum(t + w // 2, seq) - jnp.maximum(t - w // 2, 0)).astype(F32)
                pooled_ref[rows, :] = (acc / cnt - pool_ref[rows, :]).astype(pooled_ref.dtype)
            _tap_loop(pad_ref, seq, range(-(w // 2), w // 2), None, emit_pool)

    y = jnp.dot(pooled_ref[...], pw_ref[...].astype(BF16), preferred_element_type=F32)
    poolo_ref[...] = (y * ps_ref[...]).astype(poolo_ref.dtype)


def _local_mix(h, conv_dw_w, conv_dw_b, pool_w, pool_scale, sconv_w, *, layer, batch, seq):
    width = conv_dw_w.shape[2]
    ct = width // len(POOL_WINDOWS)
    nct = width // ct
    attn_cols = (KV_HEADS * Q_PER_KV + 2 * KV_HEADS) * HEAD_DIM
    c0 = attn_cols // ct
    col = lambda k: (lambda b, c: (b, c0 + k * nct + c))
    vec = lambda b, c: (layer, 0, c)
    hspec = lambda k: pl.BlockSpec((seq, ct), col(k))
    m = batch * seq
    out_spec = pl.BlockSpec((seq, ct), lambda b, c: (b, c))
    body = functools.partial(_local_body, seq=seq)
    return pl.pallas_call(
        body,
        grid=(batch, nct),
        in_specs=[
            hspec(0), hspec(1), hspec(2), hspec(3), hspec(4), hspec(5),
            pl.BlockSpec((None, CONV_KERNEL, ct), vec),
            pl.BlockSpec((None, 1, ct), vec),
            pl.BlockSpec((None, None, ct, ct), lambda b, c: (layer, c, 0, 0)),
            pl.BlockSpec((None, 1, ct), vec),
            pl.BlockSpec((None, SCONV_KERNEL, ct), vec),
        ],
        out_specs=[out_spec, out_spec, out_spec],
        out_shape=[jax.ShapeDtypeStruct((m, width), F32),
                   jax.ShapeDtypeStruct((m, width), BF16),
                   jax.ShapeDtypeStruct((m, width), BF16)],
        scratch_shapes=[pltpu.VMEM((seq + 2 * _PAD, ct), F32), pltpu.VMEM((seq, ct), BF16)],
        compiler_params=_params(
            ("parallel", "arbitrary"),
            _vmem_limit([((seq, ct), F32)] * 7 + [((seq, ct), BF16)] * 2,
                        [((seq + 2 * _PAD, ct), F32), ((seq, ct), BF16), ((seq, ct), F32)])),
        name="local_mix",
    )(h, h, h, h, h, h, conv_dw_w, conv_dw_b, pool_w, pool_scale, sconv_w)


OUT_K_SPLIT = 2


def _out_proj_body(x_ref, mod_ref, attn_ref, conv_ref, pool_ref, s_ref, cg_ref, cb_ref, w_ref, lng_ref, lnb_ref,
                   o_ref, a_ref, *, alpha):
    k = pl.program_id(1)
    kc = w_ref.shape[0]

    @pl.when(k == OUT_K_SPLIT)
    def _():
        g = cg_ref[...]
        b = cb_ref[...]

        def chunk(rows):
            a_ref[rows, :] = _silu(_normalize(conv_ref[rows, :]) * g + b).astype(a_ref.dtype)
        _row_chunks(conv_ref.shape[0], chunk)

    for piece, src_ref in enumerate((attn_ref, a_ref, pool_ref, s_ref)):
        for part in range(OUT_K_SPLIT):
            step = piece * OUT_K_SPLIT + part

            @pl.when(k == step)
            def _(src_ref=src_ref, part=part, step=step):
                contrib = jnp.dot(src_ref[:, pl.ds(part * kc, kc)], w_ref[...], preferred_element_type=F32)
                if step == 0:
                    o_ref[...] = contrib
                else:
                    o_ref[...] += contrib

    @pl.when(k == pl.num_programs(1) - 1)
    def _():
        _residual_norm_into(x_ref, mod_ref, 1, 1.0, alpha, lng_ref, lnb_ref, o_ref)


def _out_proj(x, mod, attn, conv, pool, s, conv_ln_g, conv_ln_b, w_out, ln_g, ln_b, *, layer, alpha, seq, tm=512):
    m, d = x.shape
    width = attn.shape[1]
    kc = width // OUT_K_SPLIT
    assert m % tm == 0 and w_out.shape[1] == 4 * width and (seq % tm == 0 or mod.shape[0] == 1)
    piece = pl.BlockSpec((tm, width), lambda i, k: (i, 0))
    body = functools.partial(_out_proj_body, alpha=alpha)
    return pl.pallas_call(
        body,
        grid=(m // tm, 4 * OUT_K_SPLIT),
        in_specs=[
            pl.BlockSpec((tm, d), lambda i, k: (i, 0)),
            pl.BlockSpec((None, N_MOD, d), _mod_index(mod.shape[0], max(seq // tm, 1))),
            piece, piece, piece, piece,
            pl.BlockSpec((None, 1, width), lambda i, k: (layer, 0, 0)),
            pl.BlockSpec((None, 1, width), lambda i, k: (layer, 0, 0)),
            pl.BlockSpec((None, kc, d), lambda i, k: (layer, k, 0)),
            pl.BlockSpec((None, 1, d), lambda i, k: (layer * 3 + 1, 0, 0)),
            pl.BlockSpec((None, 1, d), lambda i, k: (layer * 3 + 1, 0, 0)),
        ],
        out_specs=pl.BlockSpec((tm, d), lambda i, k: (i, 0)),
        out_shape=jax.ShapeDtypeStruct((m, d), F32),
        scratch_shapes=[pltpu.VMEM((tm, width), BF16)],
        compiler_params=_params(
            ("parallel", "arbitrary"),
            _vmem_limit([((tm, d), F32), ((tm, d), F32), ((kc, d), BF16), ((tm, width), F32),
                         ((tm, width), BF16), ((tm, width), BF16), ((tm, width), BF16), ((N_MOD, d), F32)],
                        [((tm, width), BF16), ((tm, d), F32)])),
        name="out_proj",
    )(x, mod, attn, conv, pool, s, conv_ln_g, conv_ln_b, w_out, ln_g, ln_b)


def kernel(x_prompt, x_sample, cache_k, cache_v, c, c_ctx, w_mod, b_mod, ln_g, ln_b, ffn_w_gu, ffn_w_down, w_in,
           q_norm_g, k_norm_g, conv_dw_w, conv_dw_b, conv_ln_g, conv_ln_b, pool_w, pool_scale, sconv_w, w_out):
    depth = w_mod.shape[0]
    d = x_prompt.shape[-1]
    alpha = float((2 * depth) ** 0.25)
    n_dec = c.shape[0]

    w_gu = ffn_w_gu.astype(BF16)
    w_down = ffn_w_down.astype(BF16)
    w_in_b = w_in.astype(BF16)
    w_out_b = w_out.astype(BF16)

    n_rows = 1 + n_dec
    padded = -(-n_rows // V7X_SUBLANES) * V7X_SUBLANES
    cvec = jnp.concatenate([c_ctx[None, :], c, jnp.zeros((padded - n_rows, d), F32)], axis=0)
    mod_all = _modulation(cvec, w_mod, b_mod).reshape(depth, padded, N_MOD, d)

    ln_g3 = ln_g.reshape(depth * 3, 1, d)
    ln_b3 = ln_b.reshape(depth * 3, 1, d)
    vec3 = lambda a: a.reshape(depth, 1, a.shape[-1])
    gq, gk = vec3(q_norm_g), vec3(k_norm_g)
    cb, cg, cbias, ps = vec3(conv_ln_b), vec3(conv_ln_g), vec3(conv_dw_b), vec3(pool_scale)

    def run_group(x3, mods, ctx_k, ctx_v, emit_state):
        batch, seq, _ = x3.shape
        x = x3.reshape(batch * seq, d)
        states = []
        for l in range(depth):
            mod = mods[l]
            common = dict(layer=l, seq=seq)
            x = _ffn(x, mod, w_gu, w_down, ln_g3, ln_b3, ffn_idx=0, sub=0, alpha=alpha, **common)
            h = _in_proj(x, mod, w_in_b, **common)
            attn = _attention(h, gq, gk, layer=l, batch=batch, seq=seq, ctx_k=ctx_k, ctx_v=ctx_v,
                              emit_state=emit_state)
            if emit_state:
                attn, k_l, v_l = attn
                states.append((k_l, v_l))
            else:
                attn, = attn
            conv, pool, s = _local_mix(h, conv_dw_w, cbias, pool_w, ps, sconv_w, layer=l, batch=batch, seq=seq)
            x = _out_proj(x, mod, attn, conv, pool, s, cg, cb, w_out_b, ln_g3, ln_b3, alpha=alpha, **common)
            x = _ffn(x, mod, w_gu, w_down, ln_g3, ln_b3, ffn_idx=1, sub=2, alpha=alpha, **common)
        return x.reshape(batch, seq, d), states

    mods_ctx = [mod_all[l, 0:1] for l in range(depth)]
    mods_dec = [mod_all[l, 1:1 + n_dec] for l in range(depth)]
    y_prompt, states = run_group(x_prompt, mods_ctx, None, None, True)
    y_sample, _ = run_group(x_sample, mods_dec, cache_k, cache_v, False)
    state_k = jnp.stack([k for k, _ in states], axis=1)
    state_v = jnp.stack([v for _, v in states], axis=1)
    return y_prompt, y_sample, state_k, state_v
```

```python
import functools
import math

import jax
import jax.numpy as jnp
from jax import lax
from jax.experimental import pallas as pl
from jax.experimental.pallas import tpu as pltpu

GRID_W = 64
HEAD_DIM = 128
KV_HEADS = 2
Q_PER_KV = 4
CONV_KERNEL = 31
POOL_WINDOWS = (2, 4, 8, 16)
SCONV_KERNEL = 3
ROPE_THETA = 10000.0
N_MOD = 9
EPS = 1e-6

V7X_VMEM_BYTES = 64 * 1024 * 1024
V7X_SUBLANES = 8
V7X_LANES = 128

BF16 = jnp.bfloat16
F32 = jnp.float32


def _nbytes(shape, dtype):
    return math.prod(shape) * jnp.dtype(dtype).itemsize


def _vmem_limit(double_buffered, single_buffered=()):
    need = 2 * sum(_nbytes(s, d) for s, d in double_buffered)
    need += sum(_nbytes(s, d) for s, d in single_buffered)
    need += need // 4 + (4 << 20)
    return int(min(need, V7X_VMEM_BYTES - (4 << 20)))


def _params(semantics, vmem):
    return pltpu.CompilerParams(dimension_semantics=semantics, vmem_limit_bytes=vmem)


def _silu(x):
    return x * jax.nn.sigmoid(x)


ROW_CHUNK = 2 * V7X_SUBLANES
LN_COLS = 4 * V7X_LANES


LN_UNROLL = 4


def _row_chunks(n_rows, fn):
    def body(c, carry):
        fn(pl.ds(pl.multiple_of(c * ROW_CHUNK, ROW_CHUNK), ROW_CHUNK))
        return carry
    lax.fori_loop(0, n_rows // ROW_CHUNK, body, 0, unroll=LN_UNROLL)


def _col_chunks(n_cols):
    return [pl.ds(c, LN_COLS) for c in range(0, n_cols, LN_COLS)]


def _lane_fold(x):
    parts = [x[:, c:c + V7X_LANES] for c in range(0, x.shape[1], V7X_LANES)]
    while len(parts) > 1:
        parts = [a + b for a, b in zip(parts[0::2], parts[1::2])]
    return parts[0]


def _row_sum(slabs, n_cols):
    total = None
    for x in slabs:
        folded = _lane_fold(x)
        total = folded if total is None else total + folded
    return jnp.sum(total, axis=-1, keepdims=True) * (1.0 / n_cols)


def _row_moments(load, n_cols):
    mean = _row_sum((load(cols) for cols in _col_chunks(n_cols)), n_cols)

    def centred_squares():
        for cols in _col_chunks(n_cols):
            xc = load(cols) - mean
            yield xc * xc
    var = _row_sum(centred_squares(), n_cols)
    return mean, lax.rsqrt(var + EPS)


def _modulate_into(x_ref, mod_ref, sub, xn_ref, zero_ref=None):
    n = x_ref.shape[1]

    def chunk(rows):
        mean, rstd = _row_moments(lambda cols: x_ref[rows, cols], n)
        for cols in _col_chunks(n):
            shift = mod_ref[pl.ds(3 * sub, 1), cols]
            scale1 = 1.0 + mod_ref[pl.ds(3 * sub + 1, 1), cols]
            xn_ref[rows, cols] = ((x_ref[rows, cols] - mean) * rstd * scale1 + shift).astype(xn_ref.dtype)
            if zero_ref is not None:
                zero_ref[rows, cols] = jnp.zeros((ROW_CHUNK, LN_COLS), zero_ref.dtype)
    _row_chunks(x_ref.shape[0], chunk)


def _residual_norm_into(x_ref, mod_ref, sub, branch_scale, alpha, lng_ref, lnb_ref, o_ref, z_ref):
    n = x_ref.shape[1]
    step_rows = z_ref.shape[0]
    chunks = range(step_rows // ROW_CHUNK)

    def body(c, carry):
        base = pl.multiple_of(c * step_rows, step_rows)
        rows = [pl.ds(base + k * ROW_CHUNK, ROW_CHUNK) for k in chunks]
        staged = [pl.ds(k * ROW_CHUNK, ROW_CHUNK) for k in chunks]

        def residual(k, cols):
            gate = branch_scale * mod_ref[pl.ds(3 * sub + 2, 1), cols]
            z = alpha * x_ref[rows[k], cols] + gate * o_ref[rows[k], cols]
            z_ref[staged[k], cols] = z
            return z

        def centred_squares(k, mean):
            for cols in _col_chunks(n):
                zc = z_ref[staged[k], cols] - mean
                yield zc * zc

        means = [_row_sum((residual(k, cols) for cols in _col_chunks(n)), n) for k in chunks]
        rstds = [lax.rsqrt(_row_sum(centred_squares(k, means[k]), n) + EPS) for k in chunks]
        for k in chunks:
            for cols in _col_chunks(n):
                y = (z_ref[staged[k], cols] - means[k]) * rstds[k]
                o_ref[rows[k], cols] = y * lng_ref[:, cols] + lnb_ref[:, cols]
        return carry
    lax.fori_loop(0, x_ref.shape[0] // step_rows, body, 0)


def _mod_body(c_ref, w_ref, b_ref, o_ref):
    s = _silu(c_ref[...]).astype(BF16)
    o_ref[...] = jnp.dot(s, w_ref[...].astype(BF16), preferred_element_type=F32) + b_ref[...]


def _modulation(cvec, w_mod, b_mod, *, tn=512):
    depth, d, n = w_mod.shape
    r = cvec.shape[0]
    return pl.pallas_call(
        _mod_body,
        grid=(depth, n // tn),
        in_specs=[
            pl.BlockSpec((r, d), lambda l, j: (0, 0)),
            pl.BlockSpec((None, d, tn), lambda l, j: (l, 0, j)),
            pl.BlockSpec((None, 1, tn), lambda l, j: (l, 0, j)),
        ],
        out_specs=pl.BlockSpec((None, r, tn), lambda l, j: (l, 0, j)),
        out_shape=jax.ShapeDtypeStruct((depth, r, n), F32),
        compiler_params=_params(
            ("arbitrary", "arbitrary"),
            _vmem_limit([((d, tn), F32), ((r, d), F32), ((r, tn), F32)], [((d, tn), BF16)])),
        name="modulation",
    )(cvec, w_mod, b_mod.reshape(depth, 1, n))


def _ffn_body(x_ref, mod_ref, wgu_ref, wd_ref, lng_ref, lnb_ref, o_ref, xn_ref, z_ref, *, sub, alpha):
    f = pl.program_id(1)
    tf = wd_ref.shape[0]

    @pl.when(f == 0)
    def _():
        _modulate_into(x_ref, mod_ref, sub, xn_ref, zero_ref=o_ref)

    gu = jnp.dot(xn_ref[...], wgu_ref[...], preferred_element_type=F32)
    act = (_silu(gu[:, :tf]) * gu[:, tf:]).astype(BF16)
    o_ref[...] += jnp.dot(act, wd_ref[...], preferred_element_type=F32)

    @pl.when(f == pl.num_programs(1) - 1)
    def _():
        _residual_norm_into(x_ref, mod_ref, sub, 0.5, alpha, lng_ref, lnb_ref, o_ref, z_ref)


def _mod_index(n_mod_rows, tiles_per_batch):
    if n_mod_rows == 1:
        return lambda i, j: (0, 0, 0)
    return lambda i, j: (i // tiles_per_batch, 0, 0)


FFN_TILE = 256


def _ffn_gate_up_tiles(w_gu, tf):
    depth, two, d, n2 = w_gu.shape
    ff = n2 // 2
    w = w_gu.astype(BF16).reshape(depth, two, d, 2, ff // tf, tf)
    return jnp.transpose(w, (0, 1, 4, 2, 3, 5)).reshape(depth, two, ff // tf, d, 2 * tf)


def _column_tiles(w, tn):
    depth, k, n = w.shape
    return jnp.transpose(w.astype(BF16).reshape(depth, k, n // tn, tn), (0, 2, 1, 3))


def _ffn(x, mod, w_gu, w_down, ln_g, ln_b, *, layer, ffn_idx, sub, alpha, seq, tm=512):
    m, d = x.shape
    nf, tf = w_gu.shape[2], w_gu.shape[4] // 2
    assert m % tm == 0 and w_down.shape[2] == nf * tf and (seq % tm == 0 or mod.shape[0] == 1)
    body = functools.partial(_ffn_body, sub=sub, alpha=alpha)
    return pl.pallas_call(
        body,
        grid=(m // tm, nf),
        in_specs=[
            pl.BlockSpec((tm, d), lambda i, f: (i, 0)),
            pl.BlockSpec((None, N_MOD, d), _mod_index(mod.shape[0], max(seq // tm, 1))),
            pl.BlockSpec((None, None, None, d, 2 * tf), lambda i, f: (layer, ffn_idx, f, 0, 0)),
            pl.BlockSpec((None, None, tf, d), lambda i, f: (layer, ffn_idx, f, 0)),
            pl.BlockSpec((None, 1, d), lambda i, f: (layer * 3 + sub, 0, 0)),
            pl.BlockSpec((None, 1, d), lambda i, f: (layer * 3 + sub, 0, 0)),
        ],
        out_specs=pl.BlockSpec((tm, d), lambda i, f: (i, 0)),
        out_shape=jax.ShapeDtypeStruct((m, d), F32),
        scratch_shapes=[pltpu.VMEM((tm, d), BF16), pltpu.VMEM((LN_UNROLL * ROW_CHUNK, d), F32)],
        compiler_params=_params(
            ("parallel", "arbitrary"),
            _vmem_limit([((tm, d), F32), ((tm, d), F32), ((d, 2 * tf), BF16), ((tf, d), BF16),
                         ((N_MOD, d), F32)],
                        [((tm, d), BF16), ((tm, 3 * tf), F32), ((LN_UNROLL * ROW_CHUNK, d), F32)])),
        name="ffn",
    )(x, mod, w_gu, w_down, ln_g, ln_b)


def _in_proj_body(x_ref, mod_ref, w_ref, o_ref, xn_ref):
    @pl.when(pl.program_id(1) == 0)
    def _():
        _modulate_into(x_ref, mod_ref, 1, xn_ref)

    o_ref[...] = jnp.dot(xn_ref[...], w_ref[...], preferred_element_type=F32)


IN_PROJ_TILE = 1536


def _in_proj(x, mod, w_in, *, layer, seq, tm=512):
    m, d = x.shape
    nj, tn = w_in.shape[1], w_in.shape[3]
    n = nj * tn
    assert m % tm == 0 and (seq % tm == 0 or mod.shape[0] == 1)
    return pl.pallas_call(
        _in_proj_body,
        grid=(m // tm, nj),
        in_specs=[
            pl.BlockSpec((tm, d), lambda i, j: (i, 0)),
            pl.BlockSpec((None, N_MOD, d), _mod_index(mod.shape[0], max(seq // tm, 1))),
            pl.BlockSpec((None, None, d, tn), lambda i, j: (layer, j, 0, 0)),
        ],
        out_specs=pl.BlockSpec((tm, tn), lambda i, j: (i, j)),
        out_shape=jax.ShapeDtypeStruct((m, n), F32),
        scratch_shapes=[pltpu.VMEM((tm, d), BF16)],
        compiler_params=_params(
            ("parallel", "arbitrary"),
            _vmem_limit([((tm, d), F32), ((d, tn), BF16), ((tm, tn), F32), ((N_MOD, d), F32)],
                        [((tm, d), BF16)])),
        name="in_proj",
    )(x, mod, w_in)


def _rms(x, g):
    return x * lax.rsqrt(jnp.mean(x * x, axis=-1, keepdims=True) + EPS) * g


def _rope(x, cos, sin_signed):
    q = HEAD_DIM // 4
    lane = lax.broadcasted_iota(jnp.int32, x.shape, 1)
    partner = jnp.where((lane // q) % 2 == 0,
                        pltpu.roll(x, HEAD_DIM - q, axis=1),
                        pltpu.roll(x, q, axis=1))
    return x * cos + partner * sin_signed


def _attn_body(*refs, rope, ctx, emit_state, seq):
    it = iter(refs)
    q_ref, k_ref, v_ref = next(it), next(it), next(it)
    gq_ref, gk_ref = next(it), next(it)
    if rope:
        cosq_ref, sinq_ref, cosk_ref, sink_ref = next(it), next(it), next(it), next(it)
    if ctx:
        ck_ref, cv_ref = next(it), next(it)
    o_ref = next(it)
    if emit_state:
        ko_ref, vo_ref = next(it), next(it)
    kall_ref, vall_ref = next(it), next(it)

    @pl.when(pl.program_id(2) == 0)
    def _():
        kn = _rms(k_ref[...], gk_ref[...])
        v = v_ref[...]
        if emit_state:
            ko_ref[...] = kn
            vo_ref[...] = v
        if rope:
            kn = _rope(kn, cosk_ref[...], sink_ref[...])
        kall_ref[pl.ds(0, seq), :] = kn.astype(BF16)
        vall_ref[pl.ds(0, seq), :] = v.astype(BF16)
        if ctx:
            past = ck_ref.shape[0]
            kall_ref[pl.ds(seq, past), :] = ck_ref[...].astype(BF16)
            vall_ref[pl.ds(seq, past), :] = cv_ref[...].astype(BF16)

    scale = 1.0 / math.sqrt(HEAD_DIM)
    k_all = kall_ref[...]
    v_all = vall_ref[...]
    for h in range(Q_PER_KV):
        cols = pl.ds(h * HEAD_DIM, HEAD_DIM)
        qh = _rms(q_ref[:, cols], gq_ref[...])
        if rope:
            qh = _rope(qh, cosq_ref[...], sinq_ref[...])
        s = lax.dot_general((qh * scale).astype(BF16), k_all, (((1,), (1,)), ((), ())),
                            preferred_element_type=F32)
        p = jnp.exp(s - jnp.max(s, axis=-1, keepdims=True))
        denom = jnp.sum(p, axis=-1, keepdims=True)
        o = jnp.dot(p.astype(BF16), v_all, preferred_element_type=F32) / denom
        o_ref[:, cols] = o.astype(o_ref.dtype)


def _rope_tables(seq):
    rows = seq // GRID_W
    row = jnp.repeat(jnp.arange(rows, dtype=F32), GRID_W)
    col = jnp.tile(jnp.arange(GRID_W, dtype=F32), rows)
    half = HEAD_DIM // 2
    inv = ROPE_THETA ** (-jnp.arange(0, half, 2, dtype=F32) / half)
    ar = row[:, None] * inv[None, :]
    ac = col[:, None] * inv[None, :]
    cos = jnp.concatenate([jnp.cos(ar), jnp.cos(ar), jnp.cos(ac), jnp.cos(ac)], axis=-1)
    sin = jnp.concatenate([-jnp.sin(ar), jnp.sin(ar), -jnp.sin(ac), jnp.sin(ac)], axis=-1)
    return cos, sin


def _attention(h, q_norm_g, k_norm_g, *, layer, batch, seq, ctx_k=None, ctx_v=None, emit_state, tq=256):
    rope = ctx_k is not None
    ctx = ctx_k is not None
    tq = min(tq, seq)
    nq = seq // tq
    group_w = Q_PER_KV * HEAD_DIM
    k_col0 = KV_HEADS * Q_PER_KV
    v_col0 = k_col0 + KV_HEADS
    past = ctx_k.shape[3] if ctx else 0

    in_specs = [
        pl.BlockSpec((tq, group_w), lambda b, g, i: (b * nq + i, g)),
        pl.BlockSpec((seq, HEAD_DIM), lambda b, g, i: (b, k_col0 + g)),
        pl.BlockSpec((seq, HEAD_DIM), lambda b, g, i: (b, v_col0 + g)),
        pl.BlockSpec((None, 1, HEAD_DIM), lambda b, g, i: (layer, 0, 0)),
        pl.BlockSpec((None, 1, HEAD_DIM), lambda b, g, i: (layer, 0, 0)),
    ]
    args = [h, h, h, q_norm_g, k_norm_g]
    blocks = [((tq, group_w), F32), ((seq, HEAD_DIM), F32), ((seq, HEAD_DIM), F32)]
    if rope:
        cos, sin = _rope_tables(seq)
        in_specs += [
            pl.BlockSpec((tq, HEAD_DIM), lambda b, g, i: (i, 0)),
            pl.BlockSpec((tq, HEAD_DIM), lambda b, g, i: (i, 0)),
            pl.BlockSpec((seq, HEAD_DIM), lambda b, g, i: (0, 0)),
            pl.BlockSpec((seq, HEAD_DIM), lambda b, g, i: (0, 0)),
        ]
        args += [cos, sin, cos, sin]
        blocks += [((seq, HEAD_DIM), F32)] * 2
    if ctx:
        in_specs += [pl.BlockSpec((None, None, None, past, HEAD_DIM), lambda b, g, i: (b, layer, g, 0, 0))] * 2
        args += [ctx_k, ctx_v]
        blocks += [((past, HEAD_DIM), F32)] * 2

    out_specs = [pl.BlockSpec((tq, group_w), lambda b, g, i: (b * nq + i, g))]
    out_shape = [jax.ShapeDtypeStruct((batch * seq, KV_HEADS * group_w), BF16)]
    if emit_state:
        out_specs += [pl.BlockSpec((None, None, seq, HEAD_DIM), lambda b, g, i: (b, g, 0, 0))] * 2
        out_shape += [jax.ShapeDtypeStruct((batch, KV_HEADS, seq, HEAD_DIM), F32)] * 2
        blocks += [((seq, HEAD_DIM), F32)] * 2
    total = seq + past
    body = functools.partial(_attn_body, rope=rope, ctx=ctx, emit_state=emit_state, seq=seq)
    return pl.pallas_call(
        body,
        grid=(batch, KV_HEADS, nq),
        in_specs=in_specs,
        out_specs=out_specs,
        out_shape=out_shape,
        scratch_shapes=[pltpu.VMEM((total, HEAD_DIM), BF16), pltpu.VMEM((total, HEAD_DIM), BF16)],
        compiler_params=_params(
            ("parallel", "parallel", "arbitrary"),
            _vmem_limit(blocks, [((total, HEAD_DIM), BF16)] * 2 + [((tq, total), F32)] * 3)),
        name="attention",
    )(*args)


_PAD = 16
TAP_ROWS = 32


def _tap_loop(pad_ref, seq, offsets, taps_ref, emit):
    window = TAP_ROWS + 2 * _PAD
    shifts = [_PAD + off for off in offsets]

    def body(c, carry):
        r = pl.multiple_of(c * TAP_ROWS, TAP_ROWS)
        win = pad_ref[pl.ds(r, window), :]
        acc = None
        for phase in sorted({s % V7X_SUBLANES for s in shifts}):
            rolled = win if phase == 0 else pltpu.roll(win, window - phase, axis=0)
            for j, s in enumerate(shifts):
                if s % V7X_SUBLANES != phase:
                    continue
                term = rolled[s - phase:s - phase + TAP_ROWS, :]
                if taps_ref is not None:
                    term = term * taps_ref[pl.ds(j, 1), :]
                acc = term if acc is None else acc + term
        emit(pl.ds(r, TAP_ROWS), r, acc)
        return carry
    lax.fori_loop(0, seq // TAP_ROWS, body, 0)


def _fill_loop(pad_ref, seq, fn):
    def body(c, carry):
        r = pl.multiple_of(c * TAP_ROWS, TAP_ROWS)
        pad_ref[pl.ds(r + _PAD, TAP_ROWS), :] = fn(pl.ds(r, TAP_ROWS))
        return carry
    lax.fori_loop(0, seq // TAP_ROWS, body, 0)


def _local_body(av_ref, ag_ref, pool_ref, bg_ref, cg_ref, xs_ref, cw_ref, cb_ref, pw_ref, ps_ref, sw_ref,
                conv_ref, poolo_ref, s_ref, pad_ref, pooled_ref, *, seq):
    zeros = jnp.zeros((_PAD, pad_ref.shape[1]), F32)
    pad_ref[pl.ds(0, _PAD), :] = zeros
    pad_ref[pl.ds(_PAD + seq, _PAD), :] = zeros

    _fill_loop(pad_ref, seq, lambda rows: av_ref[rows, :] * jax.nn.sigmoid(ag_ref[rows, :]))
    half = CONV_KERNEL // 2

    def emit_conv(rows, r, acc):
        conv_ref[rows, :] = acc + cb_ref[...]
    _tap_loop(pad_ref, seq, range(-half, half + 1), cw_ref, emit_conv)

    _fill_loop(pad_ref, seq, lambda rows: cg_ref[rows, :] * xs_ref[rows, :])
    half = SCONV_KERNEL // 2

    def emit_sconv(rows, r, acc):
        s_ref[rows, :] = (bg_ref[rows, :] * acc).astype(s_ref.dtype)
    _tap_loop(pad_ref, seq, range(-half, half + 1), sw_ref, emit_sconv)

    _fill_loop(pad_ref, seq, lambda rows: pool_ref[rows, :])
    group = pl.program_id(1)
    for gi, w in enumerate(POOL_WINDOWS):
        @pl.when(group == gi)
        def _(w=w):
            def emit_pool(rows, r, acc):
                t = r + lax.broadcasted_iota(jnp.int32, (TAP_ROWS, 1), 0)
                cnt = (jnp.minimum(t + w // 2, seq) - jnp.maximum(t - w // 2, 0)).astype(F32)
                pooled_ref[rows, :] = (acc / cnt - pool_ref[rows, :]).astype(pooled_ref.dtype)
            _tap_loop(pad_ref, seq, range(-(w // 2), w // 2), None, emit_pool)

    y = jnp.dot(pooled_ref[...], pw_ref[...].astype(BF16), preferred_element_type=F32)
    poolo_ref[...] = (y * ps_ref[...]).astype(poolo_ref.dtype)


def _local_mix(h, conv_dw_w, conv_dw_b, pool_w, pool_scale, sconv_w, *, layer, batch, seq):
    width = conv_dw_w.shape[2]
    ct = width // len(POOL_WINDOWS)
    nct = width // ct
    attn_cols = (KV_HEADS * Q_PER_KV + 2 * KV_HEADS) * HEAD_DIM
    c0 = attn_cols // ct
    col = lambda k: (lambda b, c: (b, c0 + k * nct + c))
    vec = lambda b, c: (layer, 0, c)
    hspec = lambda k: pl.BlockSpec((seq, ct), col(k))
    m = batch * seq
    out_spec = pl.BlockSpec((seq, ct), lambda b, c: (b, c))
    body = functools.partial(_local_body, seq=seq)
    return pl.pallas_call(
        body,
        grid=(batch, nct),
        in_specs=[
            hspec(0), hspec(1), hspec(2), hspec(3), hspec(4), hspec(5),
            pl.BlockSpec((None, CONV_KERNEL, ct), vec),
            pl.BlockSpec((None, 1, ct), vec),
            pl.BlockSpec((None, None, ct, ct), lambda b, c: (layer, c, 0, 0)),
            pl.BlockSpec((None, 1, ct), vec),
            pl.BlockSpec((None, SCONV_KERNEL, ct), vec),
        ],
        out_specs=[out_spec, out_spec, out_spec],
        out_shape=[jax.ShapeDtypeStruct((m, width), F32),
                   jax.ShapeDtypeStruct((m, width), BF16),
                   jax.ShapeDtypeStruct((m, width), BF16)],
        scratch_shapes=[pltpu.VMEM((seq + 2 * _PAD, ct), F32), pltpu.VMEM((seq, ct), BF16)],
        compiler_params=_params(
            ("parallel", "arbitrary"),
            _vmem_limit([((seq, ct), F32)] * 7 + [((seq, ct), BF16)] * 2,
                        [((seq + 2 * _PAD, ct), F32), ((seq, ct), BF16), ((seq, ct), F32)])),
        name="local_mix",
    )(h, h, h, h, h, h, conv_dw_w, conv_dw_b, pool_w, pool_scale, sconv_w)


OUT_K_SPLIT = 2


def _out_proj_body(x_ref, mod_ref, attn_ref, conv_ref, pool_ref, s_ref, cg_ref, cb_ref, w_ref, lng_ref, lnb_ref,
                   o_ref, a_ref, z_ref, *, alpha):
    k = pl.program_id(1)
    kc = w_ref.shape[0]

    @pl.when(k == OUT_K_SPLIT)
    def _():
        n = conv_ref.shape[1]

        def chunk(rows):
            mean, rstd = _row_moments(lambda cols: conv_ref[rows, cols], n)
            for cols in _col_chunks(n):
                y = (conv_ref[rows, cols] - mean) * rstd * cg_ref[:, cols] + cb_ref[:, cols]
                a_ref[rows, cols] = _silu(y).astype(a_ref.dtype)
        _row_chunks(conv_ref.shape[0], chunk)

    for piece, src_ref in enumerate((attn_ref, a_ref, pool_ref, s_ref)):
        for part in range(OUT_K_SPLIT):
            step = piece * OUT_K_SPLIT + part

            @pl.when(k == step)
            def _(src_ref=src_ref, part=part, step=step):
                contrib = jnp.dot(src_ref[:, pl.ds(part * kc, kc)], w_ref[...], preferred_element_type=F32)
                if step == 0:
                    o_ref[...] = contrib
                else:
                    o_ref[...] += contrib

    @pl.when(k == pl.num_programs(1) - 1)
    def _():
        _residual_norm_into(x_ref, mod_ref, 1, 1.0, alpha, lng_ref, lnb_ref, o_ref, z_ref)


def _out_proj(x, mod, attn, conv, pool, s, conv_ln_g, conv_ln_b, w_out, ln_g, ln_b, *, layer, alpha, seq, tm=512):
    m, d = x.shape
    width = attn.shape[1]
    kc = width // OUT_K_SPLIT
    assert m % tm == 0 and w_out.shape[1] == 4 * width and (seq % tm == 0 or mod.shape[0] == 1)
    piece = pl.BlockSpec((tm, width), lambda i, k: (i, 0))
    body = functools.partial(_out_proj_body, alpha=alpha)
    return pl.pallas_call(
        body,
        grid=(m // tm, 4 * OUT_K_SPLIT),
        in_specs=[
            pl.BlockSpec((tm, d), lambda i, k: (i, 0)),
            pl.BlockSpec((None, N_MOD, d), _mod_index(mod.shape[0], max(seq // tm, 1))),
            piece, piece, piece, piece,
            pl.BlockSpec((None, 1, width), lambda i, k: (layer, 0, 0)),
            pl.BlockSpec((None, 1, width), lambda i, k: (layer, 0, 0)),
            pl.BlockSpec((None, kc, d), lambda i, k: (layer, k, 0)),
            pl.BlockSpec((None, 1, d), lambda i, k: (layer * 3 + 1, 0, 0)),
            pl.BlockSpec((None, 1, d), lambda i, k: (layer * 3 + 1, 0, 0)),
        ],
        out_specs=pl.BlockSpec((tm, d), lambda i, k: (i, 0)),
        out_shape=jax.ShapeDtypeStruct((m, d), F32),
        scratch_shapes=[pltpu.VMEM((tm, width), BF16), pltpu.VMEM((LN_UNROLL * ROW_CHUNK, d), F32)],
        compiler_params=_params(
            ("parallel", "arbitrary"),
            _vmem_limit([((tm, d), F32), ((tm, d), F32), ((kc, d), BF16), ((tm, width), F32),
                         ((tm, width), BF16), ((tm, width), BF16), ((tm, width), BF16), ((N_MOD, d), F32)],
                        [((tm, width), BF16), ((tm, d), F32), ((LN_UNROLL * ROW_CHUNK, d), F32)])),
        name="out_proj",
    )(x, mod, attn, conv, pool, s, conv_ln_g, conv_ln_b, w_out, ln_g, ln_b)


def kernel(x_prompt, x_sample, cache_k, cache_v, c, c_ctx, w_mod, b_mod, ln_g, ln_b, ffn_w_gu, ffn_w_down, w_in,
           q_norm_g, k_norm_g, conv_dw_w, conv_dw_b, conv_ln_g, conv_ln_b, pool_w, pool_scale, sconv_w, w_out):
    depth = w_mod.shape[0]
    d = x_prompt.shape[-1]
    alpha = float((2 * depth) ** 0.25)
    n_dec = c.shape[0]

    w_gu = _ffn_gate_up_tiles(ffn_w_gu, FFN_TILE)
    w_down = ffn_w_down.astype(BF16)
    w_in_b = _column_tiles(w_in, IN_PROJ_TILE)
    w_out_b = w_out.astype(BF16)

    n_rows = 1 + n_dec
    padded = -(-n_rows // V7X_SUBLANES) * V7X_SUBLANES
    cvec = jnp.concatenate([c_ctx[None, :], c, jnp.zeros((padded - n_rows, d), F32)], axis=0)
    mod_all = _modulation(cvec, w_mod, b_mod).reshape(depth, padded, N_MOD, d)

    ln_g3 = ln_g.reshape(depth * 3, 1, d)
    ln_b3 = ln_b.reshape(depth * 3, 1, d)
    vec3 = lambda a: a.reshape(depth, 1, a.shape[-1])
    gq, gk = vec3(q_norm_g), vec3(k_norm_g)
    cb, cg, cbias, ps = vec3(conv_ln_b), vec3(conv_ln_g), vec3(conv_dw_b), vec3(pool_scale)

    def run_group(x3, mods, ctx_k, ctx_v, emit_state):
        batch, seq, _ = x3.shape
        x = x3.reshape(batch * seq, d)
        states = []
        for l in range(depth):
            mod = mods[l]
            common = dict(layer=l, seq=seq)
            x = _ffn(x, mod, w_gu, w_down, ln_g3, ln_b3, ffn_idx=0, sub=0, alpha=alpha, **common)
            h = _in_proj(x, mod, w_in_b, **common)
            attn = _attention(h, gq, gk, layer=l, batch=batch, seq=seq, ctx_k=ctx_k, ctx_v=ctx_v,
                              emit_state=emit_state)
            if emit_state:
                attn, k_l, v_l = attn
                states.append((k_l, v_l))
            else:
                attn, = attn
            conv, pool, s = _local_mix(h, conv_dw_w, cbias, pool_w, ps, sconv_w, layer=l, batch=batch, seq=seq)
            x = _out_proj(x, mod, attn, conv, pool, s, cg, cb, w_out_b, ln_g3, ln_b3, alpha=alpha, **common)
            x = _ffn(x, mod, w_gu, w_down, ln_g3, ln_b3, ffn_idx=1, sub=2, alpha=alpha, **common)
        return x.reshape(batch, seq, d), states

    mods_ctx = [mod_all[l, 0:1] for l in range(depth)]
    mods_dec = [mod_all[l, 1:1 + n_dec] for l in range(depth)]
    y_prompt, states = run_group(x_prompt, mods_ctx, None, None, True)
    y_sample, _ = run_group(x_sample, mods_dec, cache_k, cache_v, False)
    state_k = jnp.stack([k for k, _ in states], axis=1)
    state_v = jnp.stack([v for _, v in states], axis=1)
    return y_prompt, y_sample, state_k, state_v
```

```python
import functools
import math

import jax
import jax.numpy as jnp
from jax import lax
from jax.experimental import pallas as pl
from jax.experimental.pallas import tpu as pltpu

GRID_W = 64
HEAD_DIM = 128
KV_HEADS = 2
Q_PER_KV = 4
CONV_KERNEL = 31
POOL_WINDOWS = (2, 4, 8, 16)
SCONV_KERNEL = 3
ROPE_THETA = 10000.0
N_MOD = 9
EPS = 1e-6

V7X_VMEM_BYTES = 64 * 1024 * 1024
V7X_SUBLANES = 8
V7X_LANES = 128

BF16 = jnp.bfloat16
F32 = jnp.float32


def _nbytes(shape, dtype):
    return math.prod(shape) * jnp.dtype(dtype).itemsize


def _vmem_limit(double_buffered, single_buffered=()):
    need = 2 * sum(_nbytes(s, d) for s, d in double_buffered)
    need += sum(_nbytes(s, d) for s, d in single_buffered)
    need += need // 4 + (4 << 20)
    return int(min(need, V7X_VMEM_BYTES - (4 << 20)))


def _params(semantics, vmem):
    return pltpu.CompilerParams(dimension_semantics=semantics, vmem_limit_bytes=vmem)


def _silu(x):
    return x * jax.nn.sigmoid(x)


ROW_CHUNK = 2 * V7X_SUBLANES
LN_COLS = 4 * V7X_LANES


LN_UNROLL = 4


def _row_chunks(n_rows, fn):
    def body(c, carry):
        fn(pl.ds(pl.multiple_of(c * ROW_CHUNK, ROW_CHUNK), ROW_CHUNK))
        return carry
    lax.fori_loop(0, n_rows // ROW_CHUNK, body, 0, unroll=LN_UNROLL)


def _col_chunks(n_cols):
    return [pl.ds(c, LN_COLS) for c in range(0, n_cols, LN_COLS)]


def _lane_fold(x):
    parts = [x[:, c:c + V7X_LANES] for c in range(0, x.shape[1], V7X_LANES)]
    while len(parts) > 1:
        parts = [a + b for a, b in zip(parts[0::2], parts[1::2])]
    return parts[0]


def _row_sum(slabs, n_cols):
    total = None
    for x in slabs:
        folded = _lane_fold(x)
        total = folded if total is None else total + folded
    return jnp.sum(total, axis=-1, keepdims=True) * (1.0 / n_cols)


def _row_moments(load, n_cols):
    mean = _row_sum((load(cols) for cols in _col_chunks(n_cols)), n_cols)

    def centred_squares():
        for cols in _col_chunks(n_cols):
            xc = load(cols) - mean
            yield xc * xc
    var = _row_sum(centred_squares(), n_cols)
    return mean, lax.rsqrt(var + EPS)


def _modulate_into(x_ref, mod_ref, sub, xn_ref, zero_ref=None, out_row0=0):
    n = x_ref.shape[1]

    def chunk(rows):
        out_rows = pl.ds(pl.multiple_of(out_row0 + rows.start, ROW_CHUNK), ROW_CHUNK)
        mean, rstd = _row_moments(lambda cols: x_ref[rows, cols], n)
        for cols in _col_chunks(n):
            shift = mod_ref[pl.ds(3 * sub, 1), cols]
            scale1 = 1.0 + mod_ref[pl.ds(3 * sub + 1, 1), cols]
            xn_ref[out_rows, cols] = ((x_ref[rows, cols] - mean) * rstd * scale1 + shift).astype(xn_ref.dtype)
            if zero_ref is not None:
                zero_ref[out_rows, cols] = jnp.zeros((ROW_CHUNK, LN_COLS), zero_ref.dtype)
    _row_chunks(x_ref.shape[0], chunk)


def _residual_norm_into(x_ref, mod_ref, sub, branch_scale, alpha, lng_ref, lnb_ref, acc_ref, out_ref, z_ref,
                        acc_row0=0):
    n = x_ref.shape[1]
    step_rows = z_ref.shape[0]
    chunks = range(step_rows // ROW_CHUNK)

    def body(c, carry):
        base = pl.multiple_of(c * step_rows, step_rows)
        acc_base = pl.multiple_of(acc_row0 + base, step_rows)
        rows = [pl.ds(base + k * ROW_CHUNK, ROW_CHUNK) for k in chunks]
        acc_rows = [pl.ds(acc_base + k * ROW_CHUNK, ROW_CHUNK) for k in chunks]
        staged = [pl.ds(k * ROW_CHUNK, ROW_CHUNK) for k in chunks]

        def residual(k, cols):
            gate = branch_scale * mod_ref[pl.ds(3 * sub + 2, 1), cols]
            z = alpha * x_ref[rows[k], cols] + gate * acc_ref[acc_rows[k], cols]
            z_ref[staged[k], cols] = z
            return z

        def centred_squares(k, mean):
            for cols in _col_chunks(n):
                zc = z_ref[staged[k], cols] - mean
                yield zc * zc

        means = [_row_sum((residual(k, cols) for cols in _col_chunks(n)), n) for k in chunks]
        rstds = [lax.rsqrt(_row_sum(centred_squares(k, means[k]), n) + EPS) for k in chunks]
        for k in chunks:
            for cols in _col_chunks(n):
                y = (z_ref[staged[k], cols] - means[k]) * rstds[k]
                out_ref[rows[k], cols] = y * lng_ref[:, cols] + lnb_ref[:, cols]
        return carry
    lax.fori_loop(0, x_ref.shape[0] // step_rows, body, 0)


def _mod_body(c_ref, w_ref, b_ref, o_ref):
    s = _silu(c_ref[...]).astype(BF16)
    o_ref[...] = jnp.dot(s, w_ref[...].astype(BF16), preferred_element_type=F32) + b_ref[...]


def _modulation(cvec, w_mod, b_mod, *, tn=512):
    depth, d, n = w_mod.shape
    r = cvec.shape[0]
    return pl.pallas_call(
        _mod_body,
        grid=(depth, n // tn),
        in_specs=[
            pl.BlockSpec((r, d), lambda l, j: (0, 0)),
            pl.BlockSpec((None, d, tn), lambda l, j: (l, 0, j)),
            pl.BlockSpec((None, 1, tn), lambda l, j: (l, 0, j)),
        ],
        out_specs=pl.BlockSpec((None, r, tn), lambda l, j: (l, 0, j)),
        out_shape=jax.ShapeDtypeStruct((depth, r, n), F32),
        compiler_params=_params(
            ("arbitrary", "arbitrary"),
            _vmem_limit([((d, tn), F32), ((r, d), F32), ((r, tn), F32)], [((d, tn), BF16)])),
        name="modulation",
    )(cvec, w_mod, b_mod.reshape(depth, 1, n))


FFN_ROWS = 1024
FFN_TILE = 256
SLAB = 128
X_SLOTS = 3
O_SLOTS = 2


def _ffn_body(x_hbm, mod_ref, wg_ref, wu_ref, wd_ref, lng_ref, lnb_ref, o_hbm,
              acc_ref, xn_ref, xbuf, obuf, z_ref, xsem, osem, *, sub, alpha):
    i, f = pl.program_id(0), pl.program_id(1)
    last = pl.num_programs(1) - 1
    tm = acc_ref.shape[0]
    n_slabs = tm // SLAB
    row0 = i * tm

    def x_copy(s, slot):
        rows = pl.ds(pl.multiple_of(row0 + s * SLAB, SLAB), SLAB)
        return pltpu.make_async_copy(x_hbm.at[rows, :], xbuf.at[slot], xsem.at[slot])

    def o_copy(s, slot):
        rows = pl.ds(pl.multiple_of(row0 + s * SLAB, SLAB), SLAB)
        return pltpu.make_async_copy(obuf.at[slot], o_hbm.at[rows, :], osem.at[slot])

    def prefetch_x():
        for s in range(min(X_SLOTS - 1, n_slabs)):
            x_copy(s, s).start()

    def next_x(s):
        slot = lax.rem(s, X_SLOTS)
        x_copy(s, slot).wait()
        ahead = s + (X_SLOTS - 1)

        @pl.when(ahead < n_slabs)
        def _():
            x_copy(ahead, lax.rem(ahead, X_SLOTS)).start()
        return slot

    @pl.when(f == 0)
    def _():
        prefetch_x()

        def slab(s, carry):
            slot = next_x(s)
            _modulate_into(xbuf.at[slot], mod_ref, sub, xn_ref, zero_ref=acc_ref,
                           out_row0=pl.multiple_of(s * SLAB, SLAB))
            return carry
        lax.fori_loop(0, n_slabs, slab, 0)

    @pl.when(f == last)
    def _():
        prefetch_x()

    xn = xn_ref[...]
    g = jnp.dot(xn, wg_ref[...], preferred_element_type=F32)
    u = jnp.dot(xn, wu_ref[...], preferred_element_type=F32)
    act = (_silu(g) * u).astype(BF16)
    acc_ref[...] += jnp.dot(act, wd_ref[...], preferred_element_type=F32)

    @pl.when(f == last)
    def _():
        def slab(s, carry):
            slot = next_x(s)
            oslot = lax.rem(s, O_SLOTS)

            @pl.when(s >= O_SLOTS)
            def _():
                o_copy(s - O_SLOTS, oslot).wait()
            _residual_norm_into(xbuf.at[slot], mod_ref, sub, 0.5, alpha, lng_ref, lnb_ref, acc_ref,
                                obuf.at[oslot], z_ref, acc_row0=pl.multiple_of(s * SLAB, SLAB))
            o_copy(s, oslot).start()
            return carry
        lax.fori_loop(0, n_slabs, slab, 0)
        for s in range(max(n_slabs - O_SLOTS, 0), n_slabs):
            o_copy(s, s % O_SLOTS).wait()


def _mod_index(n_mod_rows, tiles_per_batch):
    if n_mod_rows == 1:
        return lambda i, j: (0, 0, 0)
    return lambda i, j: (i // tiles_per_batch, 0, 0)


def _ffn_rows(m, seq, n_mod_rows):
    tm = FFN_ROWS
    while m % tm or (n_mod_rows > 1 and seq % tm):
        tm //= 2
    assert tm >= SLAB
    return tm


def _ffn(x, mod, w_gu, w_down, ln_g, ln_b, *, layer, ffn_idx, sub, alpha, seq):
    m, d = x.shape
    tf = FFN_TILE
    ff = w_down.shape[2]
    nf = ff // tf
    tm = _ffn_rows(m, seq, mod.shape[0])
    assert ff % tf == 0
    body = functools.partial(_ffn_body, sub=sub, alpha=alpha)
    return pl.pallas_call(
        body,
        grid=(m // tm, nf),
        in_specs=[
            pl.BlockSpec(memory_space=pl.ANY),
            pl.BlockSpec((None, N_MOD, d), _mod_index(mod.shape[0], max(seq // tm, 1))),
            pl.BlockSpec((None, None, d, tf), lambda i, f: (layer, ffn_idx, 0, f)),
            pl.BlockSpec((None, None, d, tf), lambda i, f: (layer, ffn_idx, 0, nf + f)),
            pl.BlockSpec((None, None, tf, d), lambda i, f: (layer, ffn_idx, f, 0)),
            pl.BlockSpec((None, 1, d), lambda i, f: (layer * 3 + sub, 0, 0)),
            pl.BlockSpec((None, 1, d), lambda i, f: (layer * 3 + sub, 0, 0)),
        ],
        out_specs=pl.BlockSpec(memory_space=pl.ANY),
        out_shape=jax.ShapeDtypeStruct((m, d), F32),
        scratch_shapes=[
            pltpu.VMEM((tm, d), F32),
            pltpu.VMEM((tm, d), BF16),
            pltpu.VMEM((X_SLOTS, SLAB, d), F32),
            pltpu.VMEM((O_SLOTS, SLAB, d), F32),
            pltpu.VMEM((LN_UNROLL * ROW_CHUNK, d), F32),
            pltpu.SemaphoreType.DMA((X_SLOTS,)),
            pltpu.SemaphoreType.DMA((O_SLOTS,)),
        ],
        compiler_params=_params(
            ("arbitrary", "arbitrary"),
            _vmem_limit([((d, 2 * tf), BF16), ((tf, d), BF16), ((N_MOD, d), F32)],
                        [((tm, d), F32), ((tm, d), BF16), ((X_SLOTS + O_SLOTS, SLAB, d), F32),
                         ((tm, 3 * tf), F32), ((LN_UNROLL * ROW_CHUNK, d), F32)])),
        name="ffn",
    )(x, mod, w_gu, w_gu, w_down, ln_g, ln_b)


def _in_proj_body(x_ref, mod_ref, w_ref, o_ref, xn_ref):
    @pl.when(pl.program_id(1) == 0)
    def _():
        _modulate_into(x_ref, mod_ref, 1, xn_ref)

    o_ref[...] = jnp.dot(xn_ref[...], w_ref[...], preferred_element_type=F32)


IN_PROJ_TILE = 1536


def _in_proj(x, mod, w_in, *, layer, seq, tm=512, tn=IN_PROJ_TILE):
    m, d = x.shape
    n = w_in.shape[2]
    assert m % tm == 0 and n % tn == 0 and (seq % tm == 0 or mod.shape[0] == 1)
    return pl.pallas_call(
        _in_proj_body,
        grid=(m // tm, n // tn),
        in_specs=[
            pl.BlockSpec((tm, d), lambda i, j: (i, 0)),
            pl.BlockSpec((None, N_MOD, d), _mod_index(mod.shape[0], max(seq // tm, 1))),
            pl.BlockSpec((None, d, tn), lambda i, j: (layer, 0, j)),
        ],
        out_specs=pl.BlockSpec((tm, tn), lambda i, j: (i, j)),
        out_shape=jax.ShapeDtypeStruct((m, n), F32),
        scratch_shapes=[pltpu.VMEM((tm, d), BF16)],
        compiler_params=_params(
            ("parallel", "arbitrary"),
            _vmem_limit([((tm, d), F32), ((d, tn), BF16), ((tm, tn), F32), ((N_MOD, d), F32)],
                        [((tm, d), BF16)])),
        name="in_proj",
    )(x, mod, w_in)


def _rms(x, g):
    return x * lax.rsqrt(jnp.mean(x * x, axis=-1, keepdims=True) + EPS) * g


def _rope(x, cos, sin_signed):
    q = HEAD_DIM // 4
    lane = lax.broadcasted_iota(jnp.int32, x.shape, 1)
    partner = jnp.where((lane // q) % 2 == 0,
                        pltpu.roll(x, HEAD_DIM - q, axis=1),
                        pltpu.roll(x, q, axis=1))
    return x * cos + partner * sin_signed


def _attn_body(*refs, rope, ctx, emit_state, seq):
    it = iter(refs)
    q_ref, k_ref, v_ref = next(it), next(it), next(it)
    gq_ref, gk_ref = next(it), next(it)
    if rope:
        cosq_ref, sinq_ref, cosk_ref, sink_ref = next(it), next(it), next(it), next(it)
    if ctx:
        ck_ref, cv_ref = next(it), next(it)
    o_ref = next(it)
    if emit_state:
        ko_ref, vo_ref = next(it), next(it)
    kall_ref, vall_ref = next(it), next(it)

    @pl.when(pl.program_id(2) == 0)
    def _():
        kn = _rms(k_ref[...], gk_ref[...])
        v = v_ref[...]
        if emit_state:
            ko_ref[...] = kn
            vo_ref[...] = v
        if rope:
            kn = _rope(kn, cosk_ref[...], sink_ref[...])
        kall_ref[pl.ds(0, seq), :] = kn.astype(BF16)
        vall_ref[pl.ds(0, seq), :] = v.astype(BF16)
        if ctx:
            past = ck_ref.shape[0]
            kall_ref[pl.ds(seq, past), :] = ck_ref[...].astype(BF16)
            vall_ref[pl.ds(seq, past), :] = cv_ref[...].astype(BF16)

    scale = 1.0 / math.sqrt(HEAD_DIM)
    k_all = kall_ref[...]
    v_all = vall_ref[...]
    for h in range(Q_PER_KV):
        cols = pl.ds(h * HEAD_DIM, HEAD_DIM)
        qh = _rms(q_ref[:, cols], gq_ref[...])
        if rope:
            qh = _rope(qh, cosq_ref[...], sinq_ref[...])
        s = lax.dot_general((qh * scale).astype(BF16), k_all, (((1,), (1,)), ((), ())),
                            preferred_element_type=F32)
        p = jnp.exp(s - jnp.max(s, axis=-1, keepdims=True))
        denom = jnp.sum(p, axis=-1, keepdims=True)
        o = jnp.dot(p.astype(BF16), v_all, preferred_element_type=F32) / denom
        o_ref[:, cols] = o.astype(o_ref.dtype)


def _rope_tables(seq):
    rows = seq // GRID_W
    row = jnp.repeat(jnp.arange(rows, dtype=F32), GRID_W)
    col = jnp.tile(jnp.arange(GRID_W, dtype=F32), rows)
    half = HEAD_DIM // 2
    inv = ROPE_THETA ** (-jnp.arange(0, half, 2, dtype=F32) / half)
    ar = row[:, None] * inv[None, :]
    ac = col[:, None] * inv[None, :]
    cos = jnp.concatenate([jnp.cos(ar), jnp.cos(ar), jnp.cos(ac), jnp.cos(ac)], axis=-1)
    sin = jnp.concatenate([-jnp.sin(ar), jnp.sin(ar), -jnp.sin(ac), jnp.sin(ac)], axis=-1)
    return cos, sin


def _attention(h, q_norm_g, k_norm_g, *, layer, batch, seq, ctx_k=None, ctx_v=None, emit_state, tq=256):
    rope = ctx_k is not None
    ctx = ctx_k is not None
    tq = min(tq, seq)
    nq = seq // tq
    group_w = Q_PER_KV * HEAD_DIM
    k_col0 = KV_HEADS * Q_PER_KV
    v_col0 = k_col0 + KV_HEADS
    past = ctx_k.shape[3] if ctx else 0

    in_specs = [
        pl.BlockSpec((tq, group_w), lambda b, g, i: (b * nq + i, g)),
        pl.BlockSpec((seq, HEAD_DIM), lambda b, g, i: (b, k_col0 + g)),
        pl.BlockSpec((seq, HEAD_DIM), lambda b, g, i: (b, v_col0 + g)),
        pl.BlockSpec((None, 1, HEAD_DIM), lambda b, g, i: (layer, 0, 0)),
        pl.BlockSpec((None, 1, HEAD_DIM), lambda b, g, i: (layer, 0, 0)),
    ]
    args = [h, h, h, q_norm_g, k_norm_g]
    blocks = [((tq, group_w), F32), ((seq, HEAD_DIM), F32), ((seq, HEAD_DIM), F32)]
    if rope:
        cos, sin = _rope_tables(seq)
        in_specs += [
            pl.BlockSpec((tq, HEAD_DIM), lambda b, g, i: (i, 0)),
            pl.BlockSpec((tq, HEAD_DIM), lambda b, g, i: (i, 0)),
            pl.BlockSpec((seq, HEAD_DIM), lambda b, g, i: (0, 0)),
            pl.BlockSpec((seq, HEAD_DIM), lambda b, g, i: (0, 0)),
        ]
        args += [cos, sin, cos, sin]
        blocks += [((seq, HEAD_DIM), F32)] * 2
    if ctx:
        in_specs += [pl.BlockSpec((None, None, None, past, HEAD_DIM), lambda b, g, i: (b, layer, g, 0, 0))] * 2
        args += [ctx_k, ctx_v]
        blocks += [((past, HEAD_DIM), F32)] * 2

    out_specs = [pl.BlockSpec((tq, group_w), lambda b, g, i: (b * nq + i, g))]
    out_shape = [jax.ShapeDtypeStruct((batch * seq, KV_HEADS * group_w), BF16)]
    if emit_state:
        out_specs += [pl.BlockSpec((None, None, seq, HEAD_DIM), lambda b, g, i: (b, g, 0, 0))] * 2
        out_shape += [jax.ShapeDtypeStruct((batch, KV_HEADS, seq, HEAD_DIM), F32)] * 2
        blocks += [((seq, HEAD_DIM), F32)] * 2
    total = seq + past
    body = functools.partial(_attn_body, rope=rope, ctx=ctx, emit_state=emit_state, seq=seq)
    return pl.pallas_call(
        body,
        grid=(batch, KV_HEADS, nq),
        in_specs=in_specs,
        out_specs=out_specs,
        out_shape=out_shape,
        scratch_shapes=[pltpu.VMEM((total, HEAD_DIM), BF16), pltpu.VMEM((total, HEAD_DIM), BF16)],
        compiler_params=_params(
            ("parallel", "parallel", "arbitrary"),
            _vmem_limit(blocks, [((total, HEAD_DIM), BF16)] * 2 + [((tq, total), F32)] * 3)),
        name="attention",
    )(*args)


_PAD = 16
TAP_ROWS = 32


def _tap_loop(pad_ref, seq, offsets, taps_ref, emit):
    window = TAP_ROWS + 2 * _PAD
    shifts = [_PAD + off for off in offsets]

    def body(c, carry):
        r = pl.multiple_of(c * TAP_ROWS, TAP_ROWS)
        win = pad_ref[pl.ds(r, window), :]
        acc = None
        for phase in sorted({s % V7X_SUBLANES for s in shifts}):
            rolled = win if phase == 0 else pltpu.roll(win, window - phase, axis=0)
            for j, s in enumerate(shifts):
                if s % V7X_SUBLANES != phase:
                    continue
                term = rolled[s - phase:s - phase + TAP_ROWS, :]
                if taps_ref is not None:
                    term = term * taps_ref[pl.ds(j, 1), :]
                acc = term if acc is None else acc + term
        emit(pl.ds(r, TAP_ROWS), r, acc)
        return carry
    lax.fori_loop(0, seq // TAP_ROWS, body, 0)


def _fill_loop(pad_ref, seq, fn):
    def body(c, carry):
        r = pl.multiple_of(c * TAP_ROWS, TAP_ROWS)
        pad_ref[pl.ds(r + _PAD, TAP_ROWS), :] = fn(pl.ds(r, TAP_ROWS))
        return carry
    lax.fori_loop(0, seq // TAP_ROWS, body, 0)


def _local_body(av_ref, ag_ref, pool_ref, bg_ref, cg_ref, xs_ref, cw_ref, cb_ref, pw_ref, ps_ref, sw_ref,
                conv_ref, poolo_ref, s_ref, pad_ref, pooled_ref, *, seq):
    zeros = jnp.zeros((_PAD, pad_ref.shape[1]), F32)
    pad_ref[pl.ds(0, _PAD), :] = zeros
    pad_ref[pl.ds(_PAD + seq, _PAD), :] = zeros

    _fill_loop(pad_ref, seq, lambda rows: av_ref[rows, :] * jax.nn.sigmoid(ag_ref[rows, :]))
    half = CONV_KERNEL // 2

    def emit_conv(rows, r, acc):
        conv_ref[rows, :] = acc + cb_ref[...]
    _tap_loop(pad_ref, seq, range(-half, half + 1), cw_ref, emit_conv)

    _fill_loop(pad_ref, seq, lambda rows: cg_ref[rows, :] * xs_ref[rows, :])
    half = SCONV_KERNEL // 2

    def emit_sconv(rows, r, acc):
        s_ref[rows, :] = (bg_ref[rows, :] * acc).astype(s_ref.dtype)
    _tap_loop(pad_ref, seq, range(-half, half + 1), sw_ref, emit_sconv)

    _fill_loop(pad_ref, seq, lambda rows: pool_ref[rows, :])
    group = pl.program_id(1)
    for gi, w in enumerate(POOL_WINDOWS):
        @pl.when(group == gi)
        def _(w=w):
            def emit_pool(rows, r, acc):
                t = r + lax.broadcasted_iota(jnp.int32, (TAP_ROWS, 1), 0)
                cnt = (jnp.minimum(t + w // 2, seq) - jnp.maximum(t - w // 2, 0)).astype(F32)
                pooled_ref[rows, :] = (acc / cnt - pool_ref[rows, :]).astype(pooled_ref.dtype)
            _tap_loop(pad_ref, seq, range(-(w // 2), w // 2), None, emit_pool)

    y = jnp.dot(pooled_ref[...], pw_ref[...].astype(BF16), preferred_element_type=F32)
    poolo_ref[...] = (y * ps_ref[...]).astype(poolo_ref.dtype)


def _local_mix(h, conv_dw_w, conv_dw_b, pool_w, pool_scale, sconv_w, *, layer, batch, seq):
    width = conv_dw_w.shape[2]
    ct = width // len(POOL_WINDOWS)
    nct = width // ct
    attn_cols = (KV_HEADS * Q_PER_KV + 2 * KV_HEADS) * HEAD_DIM
    c0 = attn_cols // ct
    col = lambda k: (lambda b, c: (b, c0 + k * nct + c))
    vec = lambda b, c: (layer, 0, c)
    hspec = lambda k: pl.BlockSpec((seq, ct), col(k))
    m = batch * seq
    out_spec = pl.BlockSpec((seq, ct), lambda b, c: (b, c))
    body = functools.partial(_local_body, seq=seq)
    return pl.pallas_call(
        body,
        grid=(batch, nct),
        in_specs=[
            hspec(0), hspec(1), hspec(2), hspec(3), hspec(4), hspec(5),
            pl.BlockSpec((None, CONV_KERNEL, ct), vec),
            pl.BlockSpec((None, 1, ct), vec),
            pl.BlockSpec((None, None, ct, ct), lambda b, c: (layer, c, 0, 0)),
            pl.BlockSpec((None, 1, ct), vec),
            pl.BlockSpec((None, SCONV_KERNEL, ct), vec),
        ],
        out_specs=[out_spec, out_spec, out_spec],
        out_shape=[jax.ShapeDtypeStruct((m, width), F32),
                   jax.ShapeDtypeStruct((m, width), BF16),
                   jax.ShapeDtypeStruct((m, width), BF16)],
        scratch_shapes=[pltpu.VMEM((seq + 2 * _PAD, ct), F32), pltpu.VMEM((seq, ct), BF16)],
        compiler_params=_params(
            ("parallel", "arbitrary"),
            _vmem_limit([((seq, ct), F32)] * 7 + [((seq, ct), BF16)] * 2,
                        [((seq + 2 * _PAD, ct), F32), ((seq, ct), BF16), ((seq, ct), F32)])),
        name="local_mix",
    )(h, h, h, h, h, h, conv_dw_w, conv_dw_b, pool_w, pool_scale, sconv_w)


OUT_K_SPLIT = 2


def _out_proj_body(x_ref, mod_ref, attn_ref, conv_ref, pool_ref, s_ref, cg_ref, cb_ref, w_ref, lng_ref, lnb_ref,
                   o_ref, a_ref, z_ref, *, alpha):
    k = pl.program_id(1)
    kc = w_ref.shape[0]

    @pl.when(k == OUT_K_SPLIT)
    def _():
        n = conv_ref.shape[1]

        def chunk(rows):
            mean, rstd = _row_moments(lambda cols: conv_ref[rows, cols], n)
            for cols in _col_chunks(n):
                y = (conv_ref[rows, cols] - mean) * rstd * cg_ref[:, cols] + cb_ref[:, cols]
                a_ref[rows, cols] = _silu(y).astype(a_ref.dtype)
        _row_chunks(conv_ref.shape[0], chunk)

    for piece, src_ref in enumerate((attn_ref, a_ref, pool_ref, s_ref)):
        for part in range(OUT_K_SPLIT):
            step = piece * OUT_K_SPLIT + part

            @pl.when(k == step)
            def _(src_ref=src_ref, part=part, step=step):
                contrib = jnp.dot(src_ref[:, pl.ds(part * kc, kc)], w_ref[...], preferred_element_type=F32)
                if step == 0:
                    o_ref[...] = contrib
                else:
                    o_ref[...] += contrib

    @pl.when(k == pl.num_programs(1) - 1)
    def _():
        _residual_norm_into(x_ref, mod_ref, 1, 1.0, alpha, lng_ref, lnb_ref, o_ref, o_ref, z_ref)


def _out_proj(x, mod, attn, conv, pool, s, conv_ln_g, conv_ln_b, w_out, ln_g, ln_b, *, layer, alpha, seq, tm=512):
    m, d = x.shape
    width = attn.shape[1]
    kc = width // OUT_K_SPLIT
    assert m % tm == 0 and w_out.shape[1] == 4 * width and (seq % tm == 0 or mod.shape[0] == 1)
    piece = pl.BlockSpec((tm, width), lambda i, k: (i, 0))
    body = functools.partial(_out_proj_body, alpha=alpha)
    return pl.pallas_call(
        body,
        grid=(m // tm, 4 * OUT_K_SPLIT),
        in_specs=[
            pl.BlockSpec((tm, d), lambda i, k: (i, 0)),
            pl.BlockSpec((None, N_MOD, d), _mod_index(mod.shape[0], max(seq // tm, 1))),
            piece, piece, piece, piece,
            pl.BlockSpec((None, 1, width), lambda i, k: (layer, 0, 0)),
            pl.BlockSpec((None, 1, width), lambda i, k: (layer, 0, 0)),
            pl.BlockSpec((None, kc, d), lambda i, k: (layer, k, 0)),
            pl.BlockSpec((None, 1, d), lambda i, k: (layer * 3 + 1, 0, 0)),
            pl.BlockSpec((None, 1, d), lambda i, k: (layer * 3 + 1, 0, 0)),
        ],
        out_specs=pl.BlockSpec((tm, d), lambda i, k: (i, 0)),
        out_shape=jax.ShapeDtypeStruct((m, d), F32),
        scratch_shapes=[pltpu.VMEM((tm, width), BF16), pltpu.VMEM((LN_UNROLL * ROW_CHUNK, d), F32)],
        compiler_params=_params(
            ("parallel", "arbitrary"),
            _vmem_limit([((tm, d), F32), ((tm, d), F32), ((kc, d), BF16), ((tm, width), F32),
                         ((tm, width), BF16), ((tm, width), BF16), ((tm, width), BF16), ((N_MOD, d), F32)],
                        [((tm, width), BF16), ((tm, d), F32), ((LN_UNROLL * ROW_CHUNK, d), F32)])),
        name="out_proj",
    )(x, mod, attn, conv, pool, s, conv_ln_g, conv_ln_b, w_out, ln_g, ln_b)


def kernel(x_prompt, x_sample, cache_k, cache_v, c, c_ctx, w_mod, b_mod, ln_g, ln_b, ffn_w_gu, ffn_w_down, w_in,
           q_norm_g, k_norm_g, conv_dw_w, conv_dw_b, conv_ln_g, conv_ln_b, pool_w, pool_scale, sconv_w, w_out):
    depth = w_mod.shape[0]
    d = x_prompt.shape[-1]
    alpha = float((2 * depth) ** 0.25)
    n_dec = c.shape[0]

    w_gu = ffn_w_gu.astype(BF16)
    w_down = ffn_w_down.astype(BF16)
    w_in_b = w_in.astype(BF16)
    w_out_b = w_out.astype(BF16)

    n_rows = 1 + n_dec
    padded = -(-n_rows // V7X_SUBLANES) * V7X_SUBLANES
    cvec = jnp.concatenate([c_ctx[None, :], c, jnp.zeros((padded - n_rows, d), F32)], axis=0)
    mod_all = _modulation(cvec, w_mod, b_mod).reshape(depth, padded, N_MOD, d)

    ln_g3 = ln_g.reshape(depth * 3, 1, d)
    ln_b3 = ln_b.reshape(depth * 3, 1, d)
    vec3 = lambda a: a.reshape(depth, 1, a.shape[-1])
    gq, gk = vec3(q_norm_g), vec3(k_norm_g)
    cb, cg, cbias, ps = vec3(conv_ln_b), vec3(conv_ln_g), vec3(conv_dw_b), vec3(pool_scale)

    def run_group(x3, mods, ctx_k, ctx_v, emit_state):
        batch, seq, _ = x3.shape
        x = x3.reshape(batch * seq, d)
        states = []
        for l in range(depth):
            mod = mods[l]
            common = dict(layer=l, seq=seq)
            x = _ffn(x, mod, w_gu, w_down, ln_g3, ln_b3, ffn_idx=0, sub=0, alpha=alpha, **common)
            h = _in_proj(x, mod, w_in_b, **common)
            attn = _attention(h, gq, gk, layer=l, batch=batch, seq=seq, ctx_k=ctx_k, ctx_v=ctx_v,
                              emit_state=emit_state)
            if emit_state:
                attn, k_l, v_l = attn
                states.append((k_l, v_l))
            else:
                attn, = attn
            conv, pool, s = _local_mix(h, conv_dw_w, cbias, pool_w, ps, sconv_w, layer=l, batch=batch, seq=seq)
            x = _out_proj(x, mod, attn, conv, pool, s, cg, cb, w_out_b, ln_g3, ln_b3, alpha=alpha, **common)
            x = _ffn(x, mod, w_gu, w_down, ln_g3, ln_b3, ffn_idx=1, sub=2, alpha=alpha, **common)
        return x.reshape(batch, seq, d), states

    mods_ctx = [mod_all[l, 0:1] for l in range(depth)]
    mods_dec = [mod_all[l, 1:1 + n_dec] for l in range(depth)]
    y_prompt, states = run_group(x_prompt, mods_ctx, None, None, True)
    y_sample, _ = run_group(x_sample, mods_dec, cache_k, cache_v, False)
    state_k = jnp.stack([k for k, _ in states], axis=1)
    state_v = jnp.stack([v for _, v in states], axis=1)
    return y_prompt, y_sample, state_k, state_v
```

```python
import functools
import math

import jax
import jax.numpy as jnp
from jax import lax
from jax.experimental import pallas as pl
from jax.experimental.pallas import tpu as pltpu

GRID_W = 64
HEAD_DIM = 128
KV_HEADS = 2
Q_PER_KV = 4
CONV_KERNEL = 31
POOL_WINDOWS = (2, 4, 8, 16)
SCONV_KERNEL = 3
ROPE_THETA = 10000.0
N_MOD = 9
EPS = 1e-6

V7X_VMEM_BYTES = 64 * 1024 * 1024
V7X_SUBLANES = 8
V7X_LANES = 128

BF16 = jnp.bfloat16
F32 = jnp.float32


def _nbytes(shape, dtype):
    return math.prod(shape) * jnp.dtype(dtype).itemsize


def _vmem_limit(double_buffered, single_buffered=()):
    need = 2 * sum(_nbytes(s, d) for s, d in double_buffered)
    need += sum(_nbytes(s, d) for s, d in single_buffered)
    need += need // 4 + (4 << 20)
    return int(min(need, V7X_VMEM_BYTES - (4 << 20)))


def _params(semantics, vmem):
    return pltpu.CompilerParams(dimension_semantics=semantics, vmem_limit_bytes=vmem)


def _silu(x):
    return x * jax.nn.sigmoid(x)


ROW_CHUNK = 2 * V7X_SUBLANES
LN_COLS = 4 * V7X_LANES


LN_UNROLL = 4


def _row_chunks(n_rows, fn):
    def body(c, carry):
        fn(pl.ds(pl.multiple_of(c * ROW_CHUNK, ROW_CHUNK), ROW_CHUNK))
        return carry
    lax.fori_loop(0, n_rows // ROW_CHUNK, body, 0, unroll=LN_UNROLL)


def _col_chunks(n_cols):
    return [pl.ds(c, LN_COLS) for c in range(0, n_cols, LN_COLS)]


def _lane_fold(x):
    parts = [x[:, c:c + V7X_LANES] for c in range(0, x.shape[1], V7X_LANES)]
    while len(parts) > 1:
        parts = [a + b for a, b in zip(parts[0::2], parts[1::2])]
    return parts[0]


def _row_sum(slabs, n_cols):
    total = None
    for x in slabs:
        folded = _lane_fold(x)
        total = folded if total is None else total + folded
    return jnp.sum(total, axis=-1, keepdims=True) * (1.0 / n_cols)


def _row_moments(load, n_cols):
    mean = _row_sum((load(cols) for cols in _col_chunks(n_cols)), n_cols)

    def centred_squares():
        for cols in _col_chunks(n_cols):
            xc = load(cols) - mean
            yield xc * xc
    var = _row_sum(centred_squares(), n_cols)
    return mean, lax.rsqrt(var + EPS)


def _modulate_into(x_ref, mod_ref, sub, xn_ref, zero_ref=None, out_row0=0):
    n = x_ref.shape[1]

    def chunk(rows):
        out_rows = pl.ds(pl.multiple_of(out_row0 + rows.start, ROW_CHUNK), ROW_CHUNK)
        mean, rstd = _row_moments(lambda cols: x_ref[rows, cols], n)
        for cols in _col_chunks(n):
            shift = mod_ref[pl.ds(3 * sub, 1), cols]
            scale1 = 1.0 + mod_ref[pl.ds(3 * sub + 1, 1), cols]
            xn_ref[out_rows, cols] = ((x_ref[rows, cols] - mean) * rstd * scale1 + shift).astype(xn_ref.dtype)
            if zero_ref is not None:
                zero_ref[out_rows, cols] = jnp.zeros((ROW_CHUNK, LN_COLS), zero_ref.dtype)
    _row_chunks(x_ref.shape[0], chunk)


def _residual_norm_into(x_ref, mod_ref, sub, branch_scale, alpha, lng_ref, lnb_ref, acc_ref, out_ref, z_ref,
                        acc_row0=0):
    n = x_ref.shape[1]
    step_rows = z_ref.shape[0]
    chunks = range(step_rows // ROW_CHUNK)

    def body(c, carry):
        base = pl.multiple_of(c * step_rows, step_rows)
        acc_base = pl.multiple_of(acc_row0 + base, step_rows)
        rows = [pl.ds(base + k * ROW_CHUNK, ROW_CHUNK) for k in chunks]
        acc_rows = [pl.ds(acc_base + k * ROW_CHUNK, ROW_CHUNK) for k in chunks]
        staged = [pl.ds(k * ROW_CHUNK, ROW_CHUNK) for k in chunks]

        def residual(k, cols):
            gate = branch_scale * mod_ref[pl.ds(3 * sub + 2, 1), cols]
            z = alpha * x_ref[rows[k], cols] + gate * acc_ref[acc_rows[k], cols]
            z_ref[staged[k], cols] = z
            return z

        def centred_squares(k, mean):
            for cols in _col_chunks(n):
                zc = z_ref[staged[k], cols] - mean
                yield zc * zc

        means = [_row_sum((residual(k, cols) for cols in _col_chunks(n)), n) for k in chunks]
        rstds = [lax.rsqrt(_row_sum(centred_squares(k, means[k]), n) + EPS) for k in chunks]
        for k in chunks:
            for cols in _col_chunks(n):
                y = (z_ref[staged[k], cols] - means[k]) * rstds[k]
                out_ref[rows[k], cols] = y * lng_ref[:, cols] + lnb_ref[:, cols]
        return carry
    lax.fori_loop(0, x_ref.shape[0] // step_rows, body, 0)


def _mod_body(c_ref, w_ref, b_ref, o_ref):
    s = _silu(c_ref[...]).astype(BF16)
    o_ref[...] = jnp.dot(s, w_ref[...].astype(BF16), preferred_element_type=F32) + b_ref[...]


def _modulation(cvec, w_mod, b_mod, *, tn=512):
    depth, d, n = w_mod.shape
    r = cvec.shape[0]
    return pl.pallas_call(
        _mod_body,
        grid=(depth, n // tn),
        in_specs=[
            pl.BlockSpec((r, d), lambda l, j: (0, 0)),
            pl.BlockSpec((None, d, tn), lambda l, j: (l, 0, j)),
            pl.BlockSpec((None, 1, tn), lambda l, j: (l, 0, j)),
        ],
        out_specs=pl.BlockSpec((None, r, tn), lambda l, j: (l, 0, j)),
        out_shape=jax.ShapeDtypeStruct((depth, r, n), F32),
        compiler_params=_params(
            ("arbitrary", "arbitrary"),
            _vmem_limit([((d, tn), F32), ((r, d), F32), ((r, tn), F32)], [((d, tn), BF16)])),
        name="modulation",
    )(cvec, w_mod, b_mod.reshape(depth, 1, n))


FFN_ROWS = 1024
FFN_TILE = 256
SLAB = 64
X_SLOTS = 5
O_SLOTS = 2


def _ffn_body(x_hbm, mod_ref, wg_ref, wu_ref, wd_ref, lng_ref, lnb_ref, o_hbm,
              acc_ref, xn_ref, xbuf, obuf, z_ref, xsem, osem, *, sub, alpha):
    i, f = pl.program_id(0), pl.program_id(1)
    last = pl.num_programs(1) - 1
    tm = acc_ref.shape[0]
    n_slabs = tm // SLAB
    row0 = i * tm

    def x_copy(s, slot):
        rows = pl.ds(pl.multiple_of(row0 + s * SLAB, SLAB), SLAB)
        return pltpu.make_async_copy(x_hbm.at[rows, :], xbuf.at[slot], xsem.at[slot])

    def o_copy(s, slot):
        rows = pl.ds(pl.multiple_of(row0 + s * SLAB, SLAB), SLAB)
        return pltpu.make_async_copy(obuf.at[slot], o_hbm.at[rows, :], osem.at[slot])

    def prefetch_x():
        for s in range(min(X_SLOTS - 1, n_slabs)):
            x_copy(s, s).start()

    def next_x(s):
        slot = lax.rem(s, X_SLOTS)
        x_copy(s, slot).wait()
        ahead = s + (X_SLOTS - 1)

        @pl.when(ahead < n_slabs)
        def _():
            x_copy(ahead, lax.rem(ahead, X_SLOTS)).start()
        return slot

    @pl.when(f == 0)
    def _():
        prefetch_x()

        def slab(s, carry):
            slot = next_x(s)
            _modulate_into(xbuf.at[slot], mod_ref, sub, xn_ref, zero_ref=acc_ref,
                           out_row0=pl.multiple_of(s * SLAB, SLAB))
            return carry
        lax.fori_loop(0, n_slabs, slab, 0)

    @pl.when(f == last)
    def _():
        prefetch_x()

    xn = xn_ref[...]
    g = jnp.dot(xn, wg_ref[...].astype(BF16), preferred_element_type=F32)
    u = jnp.dot(xn, wu_ref[...].astype(BF16), preferred_element_type=F32)
    act = (_silu(g) * u).astype(BF16)
    acc_ref[...] += jnp.dot(act, wd_ref[...].astype(BF16), preferred_element_type=F32)

    @pl.when(f == last)
    def _():
        def slab(s, carry):
            slot = next_x(s)
            oslot = lax.rem(s, O_SLOTS)

            @pl.when(s >= O_SLOTS)
            def _():
                o_copy(s - O_SLOTS, oslot).wait()
            _residual_norm_into(xbuf.at[slot], mod_ref, sub, 0.5, alpha, lng_ref, lnb_ref, acc_ref,
                                obuf.at[oslot], z_ref, acc_row0=pl.multiple_of(s * SLAB, SLAB))
            o_copy(s, oslot).start()
            return carry
        lax.fori_loop(0, n_slabs, slab, 0)
        for s in range(max(n_slabs - O_SLOTS, 0), n_slabs):
            o_copy(s, s % O_SLOTS).wait()


def _mod_index(n_mod_rows, tiles_per_batch):
    if n_mod_rows == 1:
        return lambda i, j: (0, 0, 0)
    return lambda i, j: (i // tiles_per_batch, 0, 0)


def _ffn_rows(m, seq, n_mod_rows):
    tm = FFN_ROWS
    while m % tm or (n_mod_rows > 1 and seq % tm):
        tm //= 2
    assert tm >= SLAB
    return tm


def _ffn(x, mod, w_gu, w_down, ln_g, ln_b, *, layer, ffn_idx, sub, alpha, seq):
    m, d = x.shape
    tf = FFN_TILE
    ff = w_down.shape[2]
    nf = ff // tf
    tm = _ffn_rows(m, seq, mod.shape[0])
    assert ff % tf == 0
    body = functools.partial(_ffn_body, sub=sub, alpha=alpha)
    return pl.pallas_call(
        body,
        grid=(m // tm, nf),
        in_specs=[
            pl.BlockSpec(memory_space=pl.ANY),
            pl.BlockSpec((None, N_MOD, d), _mod_index(mod.shape[0], max(seq // tm, 1))),
            pl.BlockSpec((None, None, d, tf), lambda i, f: (layer, ffn_idx, 0, f)),
            pl.BlockSpec((None, None, d, tf), lambda i, f: (layer, ffn_idx, 0, nf + f)),
            pl.BlockSpec((None, None, tf, d), lambda i, f: (layer, ffn_idx, f, 0)),
            pl.BlockSpec((None, 1, d), lambda i, f: (layer * 3 + sub, 0, 0)),
            pl.BlockSpec((None, 1, d), lambda i, f: (layer * 3 + sub, 0, 0)),
        ],
        out_specs=pl.BlockSpec(memory_space=pl.ANY),
        out_shape=jax.ShapeDtypeStruct((m, d), F32),
        scratch_shapes=[
            pltpu.VMEM((tm, d), F32),
            pltpu.VMEM((tm, d), BF16),
            pltpu.VMEM((X_SLOTS, SLAB, d), F32),
            pltpu.VMEM((O_SLOTS, SLAB, d), F32),
            pltpu.VMEM((LN_UNROLL * ROW_CHUNK, d), F32),
            pltpu.SemaphoreType.DMA((X_SLOTS,)),
            pltpu.SemaphoreType.DMA((O_SLOTS,)),
        ],
        compiler_params=_params(
            ("arbitrary", "arbitrary"),
            _vmem_limit([((d, 2 * tf), w_gu.dtype), ((tf, d), w_down.dtype), ((N_MOD, d), F32)],
                        [((tm, d), F32), ((tm, d), BF16), ((X_SLOTS + O_SLOTS, SLAB, d), F32),
                         ((tm, 3 * tf), F32), ((tf, d), BF16), ((LN_UNROLL * ROW_CHUNK, d), F32)])),
        name="ffn",
    )(x, mod, w_gu, w_gu, w_down, ln_g, ln_b)


def _in_proj_body(x_ref, mod_ref, w_ref, o_ref, xn_ref):
    @pl.when(pl.program_id(1) == 0)
    def _():
        _modulate_into(x_ref, mod_ref, 1, xn_ref)

    o_ref[...] = jnp.dot(xn_ref[...], w_ref[...], preferred_element_type=F32)


IN_PROJ_TILE = 1536


def _in_proj(x, mod, w_in, *, layer, seq, tm=512, tn=IN_PROJ_TILE):
    m, d = x.shape
    n = w_in.shape[2]
    assert m % tm == 0 and n % tn == 0 and (seq % tm == 0 or mod.shape[0] == 1)
    return pl.pallas_call(
        _in_proj_body,
        grid=(m // tm, n // tn),
        in_specs=[
            pl.BlockSpec((tm, d), lambda i, j: (i, 0)),
            pl.BlockSpec((None, N_MOD, d), _mod_index(mod.shape[0], max(seq // tm, 1))),
            pl.BlockSpec((None, d, tn), lambda i, j: (layer, 0, j)),
        ],
        out_specs=pl.BlockSpec((tm, tn), lambda i, j: (i, j)),
        out_shape=jax.ShapeDtypeStruct((m, n), F32),
        scratch_shapes=[pltpu.VMEM((tm, d), BF16)],
        compiler_params=_params(
            ("parallel", "arbitrary"),
            _vmem_limit([((tm, d), F32), ((d, tn), BF16), ((tm, tn), F32), ((N_MOD, d), F32)],
                        [((tm, d), BF16)])),
        name="in_proj",
    )(x, mod, w_in)


def _rms(x, g):
    return x * lax.rsqrt(jnp.mean(x * x, axis=-1, keepdims=True) + EPS) * g


def _rope(x, cos, sin_signed):
    q = HEAD_DIM // 4
    lane = lax.broadcasted_iota(jnp.int32, x.shape, 1)
    partner = jnp.where((lane // q) % 2 == 0,
                        pltpu.roll(x, HEAD_DIM - q, axis=1),
                        pltpu.roll(x, q, axis=1))
    return x * cos + partner * sin_signed


def _attn_body(*refs, rope, ctx, emit_state, seq):
    it = iter(refs)
    q_ref, k_ref, v_ref = next(it), next(it), next(it)
    gq_ref, gk_ref = next(it), next(it)
    if rope:
        cosq_ref, sinq_ref, cosk_ref, sink_ref = next(it), next(it), next(it), next(it)
    if ctx:
        ck_ref, cv_ref = next(it), next(it)
    o_ref = next(it)
    if emit_state:
        ko_ref, vo_ref = next(it), next(it)
    kall_ref, vall_ref = next(it), next(it)

    @pl.when(pl.program_id(2) == 0)
    def _():
        kn = _rms(k_ref[...], gk_ref[...])
        v = v_ref[...]
        if emit_state:
            ko_ref[...] = kn
            vo_ref[...] = v
        if rope:
            kn = _rope(kn, cosk_ref[...], sink_ref[...])
        kall_ref[pl.ds(0, seq), :] = kn.astype(BF16)
        vall_ref[pl.ds(0, seq), :] = v.astype(BF16)
        if ctx:
            past = ck_ref.shape[0]
            kall_ref[pl.ds(seq, past), :] = ck_ref[...].astype(BF16)
            vall_ref[pl.ds(seq, past), :] = cv_ref[...].astype(BF16)

    scale = 1.0 / math.sqrt(HEAD_DIM)
    k_all = kall_ref[...]
    v_all = vall_ref[...]
    heads = [pl.ds(h * HEAD_DIM, HEAD_DIM) for h in range(Q_PER_KV)]
    scores = []
    for cols in heads:
        qh = _rms(q_ref[:, cols], gq_ref[...])
        if rope:
            qh = _rope(qh, cosq_ref[...], sinq_ref[...])
        scores.append(lax.dot_general((qh * scale).astype(BF16), k_all, (((1,), (1,)), ((), ())),
                                      preferred_element_type=F32))
    probs, denoms = [], []
    for s in scores:
        p = jnp.exp(s - jnp.max(s, axis=-1, keepdims=True))
        denoms.append(jnp.sum(p, axis=-1, keepdims=True))
        probs.append(p.astype(BF16))
    for cols, p, denom in zip(heads, probs, denoms):
        o = jnp.dot(p, v_all, preferred_element_type=F32) / denom
        o_ref[:, cols] = o.astype(o_ref.dtype)


def _rope_tables(seq):
    rows = seq // GRID_W
    row = jnp.repeat(jnp.arange(rows, dtype=F32), GRID_W)
    col = jnp.tile(jnp.arange(GRID_W, dtype=F32), rows)
    half = HEAD_DIM // 2
    inv = ROPE_THETA ** (-jnp.arange(0, half, 2, dtype=F32) / half)
    ar = row[:, None] * inv[None, :]
    ac = col[:, None] * inv[None, :]
    cos = jnp.concatenate([jnp.cos(ar), jnp.cos(ar), jnp.cos(ac), jnp.cos(ac)], axis=-1)
    sin = jnp.concatenate([-jnp.sin(ar), jnp.sin(ar), -jnp.sin(ac), jnp.sin(ac)], axis=-1)
    return cos, sin


def _attention(h, q_norm_g, k_norm_g, *, layer, batch, seq, ctx_k=None, ctx_v=None, emit_state, tq=256):
    rope = ctx_k is not None
    ctx = ctx_k is not None
    tq = min(tq, seq)
    nq = seq // tq
    group_w = Q_PER_KV * HEAD_DIM
    k_col0 = KV_HEADS * Q_PER_KV
    v_col0 = k_col0 + KV_HEADS
    past = ctx_k.shape[3] if ctx else 0

    in_specs = [
        pl.BlockSpec((tq, group_w), lambda b, g, i: (b * nq + i, g)),
        pl.BlockSpec((seq, HEAD_DIM), lambda b, g, i: (b, k_col0 + g)),
        pl.BlockSpec((seq, HEAD_DIM), lambda b, g, i: (b, v_col0 + g)),
        pl.BlockSpec((None, 1, HEAD_DIM), lambda b, g, i: (layer, 0, 0)),
        pl.BlockSpec((None, 1, HEAD_DIM), lambda b, g, i: (layer, 0, 0)),
    ]
    args = [h, h, h, q_norm_g, k_norm_g]
    blocks = [((tq, group_w), F32), ((seq, HEAD_DIM), F32), ((seq, HEAD_DIM), F32)]
    if rope:
        cos, sin = _rope_tables(seq)
        in_specs += [
            pl.BlockSpec((tq, HEAD_DIM), lambda b, g, i: (i, 0)),
            pl.BlockSpec((tq, HEAD_DIM), lambda b, g, i: (i, 0)),
            pl.BlockSpec((seq, HEAD_DIM), lambda b, g, i: (0, 0)),
            pl.BlockSpec((seq, HEAD_DIM), lambda b, g, i: (0, 0)),
        ]
        args += [cos, sin, cos, sin]
        blocks += [((seq, HEAD_DIM), F32)] * 2
    if ctx:
        in_specs += [pl.BlockSpec((None, None, None, past, HEAD_DIM), lambda b, g, i: (b, layer, g, 0, 0))] * 2
        args += [ctx_k, ctx_v]
        blocks += [((past, HEAD_DIM), F32)] * 2

    out_specs = [pl.BlockSpec((tq, group_w), lambda b, g, i: (b * nq + i, g))]
    out_shape = [jax.ShapeDtypeStruct((batch * seq, KV_HEADS * group_w), BF16)]
    if emit_state:
        out_specs += [pl.BlockSpec((None, None, seq, HEAD_DIM), lambda b, g, i: (b, g, 0, 0))] * 2
        out_shape += [jax.ShapeDtypeStruct((batch, KV_HEADS, seq, HEAD_DIM), F32)] * 2
        blocks += [((seq, HEAD_DIM), F32)] * 2
    total = seq + past
    body = functools.partial(_attn_body, rope=rope, ctx=ctx, emit_state=emit_state, seq=seq)
    return pl.pallas_call(
        body,
        grid=(batch, KV_HEADS, nq),
        in_specs=in_specs,
        out_specs=out_specs,
        out_shape=out_shape,
        scratch_shapes=[pltpu.VMEM((total, HEAD_DIM), BF16), pltpu.VMEM((total, HEAD_DIM), BF16)],
        compiler_params=_params(
            ("parallel", "parallel", "arbitrary"),
            _vmem_limit(blocks, [((total, HEAD_DIM), BF16)] * 2 + [((tq, total), F32)] * 3)),
        name="attention",
    )(*args)


_PAD = 16
TAP_ROWS = 32


def _tap_loop(pad_ref, seq, offsets, taps_ref, emit):
    window = TAP_ROWS + 2 * _PAD
    shifts = [_PAD + off for off in offsets]

    def body(c, carry):
        r = pl.multiple_of(c * TAP_ROWS, TAP_ROWS)
        win = pad_ref[pl.ds(r, window), :]
        acc = None
        for phase in sorted({s % V7X_SUBLANES for s in shifts}):
            rolled = win if phase == 0 else pltpu.roll(win, window - phase, axis=0)
            for j, s in enumerate(shifts):
                if s % V7X_SUBLANES != phase:
                    continue
                term = rolled[s - phase:s - phase + TAP_ROWS, :]
                if taps_ref is not None:
                    term = term * taps_ref[pl.ds(j, 1), :]
                acc = term if acc is None else acc + term
        emit(pl.ds(r, TAP_ROWS), r, acc)
        return carry
    lax.fori_loop(0, seq // TAP_ROWS, body, 0)


def _fill_loop(pad_ref, seq, fn):
    def body(c, carry):
        r = pl.multiple_of(c * TAP_ROWS, TAP_ROWS)
        pad_ref[pl.ds(r + _PAD, TAP_ROWS), :] = fn(pl.ds(r, TAP_ROWS))
        return carry
    lax.fori_loop(0, seq // TAP_ROWS, body, 0)


def _local_body(av_ref, ag_ref, pool_ref, bg_ref, cg_ref, xs_ref, cw_ref, cb_ref, pw_ref, ps_ref, sw_ref,
                conv_ref, poolo_ref, s_ref, pad_ref, pooled_ref, *, seq):
    zeros = jnp.zeros((_PAD, pad_ref.shape[1]), F32)
    pad_ref[pl.ds(0, _PAD), :] = zeros
    pad_ref[pl.ds(_PAD + seq, _PAD), :] = zeros

    _fill_loop(pad_ref, seq, lambda rows: av_ref[rows, :] * jax.nn.sigmoid(ag_ref[rows, :]))
    half = CONV_KERNEL // 2

    def emit_conv(rows, r, acc):
        conv_ref[rows, :] = acc + cb_ref[...]
    _tap_loop(pad_ref, seq, range(-half, half + 1), cw_ref, emit_conv)

    _fill_loop(pad_ref, seq, lambda rows: cg_ref[rows, :] * xs_ref[rows, :])
    half = SCONV_KERNEL // 2

    def emit_sconv(rows, r, acc):
        s_ref[rows, :] = (bg_ref[rows, :] * acc).astype(s_ref.dtype)
    _tap_loop(pad_ref, seq, range(-half, half + 1), sw_ref, emit_sconv)

    _fill_loop(pad_ref, seq, lambda rows: pool_ref[rows, :])
    group = pl.program_id(1)
    for gi, w in enumerate(POOL_WINDOWS):
        @pl.when(group == gi)
        def _(w=w):
            def emit_pool(rows, r, acc):
                t = r + lax.broadcasted_iota(jnp.int32, (TAP_ROWS, 1), 0)
                cnt = (jnp.minimum(t + w // 2, seq) - jnp.maximum(t - w // 2, 0)).astype(F32)
                pooled_ref[rows, :] = (acc / cnt - pool_ref[rows, :]).astype(pooled_ref.dtype)
            _tap_loop(pad_ref, seq, range(-(w // 2), w // 2), None, emit_pool)

    y = jnp.dot(pooled_ref[...], pw_ref[...].astype(BF16), preferred_element_type=F32)
    poolo_ref[...] = (y * ps_ref[...]).astype(poolo_ref.dtype)


def _local_mix(h, conv_dw_w, conv_dw_b, pool_w, pool_scale, sconv_w, *, layer, batch, seq):
    width = conv_dw_w.shape[2]
    ct = width // len(POOL_WINDOWS)
    nct = width // ct
    attn_cols = (KV_HEADS * Q_PER_KV + 2 * KV_HEADS) * HEAD_DIM
    c0 = attn_cols // ct
    col = lambda k: (lambda b, c: (b, c0 + k * nct + c))
    vec = lambda b, c: (layer, 0, c)
    hspec = lambda k: pl.BlockSpec((seq, ct), col(k))
    m = batch * seq
    out_spec = pl.BlockSpec((seq, ct), lambda b, c: (b, c))
    body = functools.partial(_local_body, seq=seq)
    return pl.pallas_call(
        body,
        grid=(batch, nct),
        in_specs=[
            hspec(0), hspec(1), hspec(2), hspec(3), hspec(4), hspec(5),
            pl.BlockSpec((None, CONV_KERNEL, ct), vec),
            pl.BlockSpec((None, 1, ct), vec),
            pl.BlockSpec((None, None, ct, ct), lambda b, c: (layer, c, 0, 0)),
            pl.BlockSpec((None, 1, ct), vec),
            pl.BlockSpec((None, SCONV_KERNEL, ct), vec),
        ],
        out_specs=[out_spec, out_spec, out_spec],
        out_shape=[jax.ShapeDtypeStruct((m, width), F32),
                   jax.ShapeDtypeStruct((m, width), BF16),
                   jax.ShapeDtypeStruct((m, width), BF16)],
        scratch_shapes=[pltpu.VMEM((seq + 2 * _PAD, ct), F32), pltpu.VMEM((seq, ct), BF16)],
        compiler_params=_params(
            ("parallel", "arbitrary"),
            _vmem_limit([((seq, ct), F32)] * 7 + [((seq, ct), BF16)] * 2,
                        [((seq + 2 * _PAD, ct), F32), ((seq, ct), BF16), ((seq, ct), F32)])),
        name="local_mix",
    )(h, h, h, h, h, h, conv_dw_w, conv_dw_b, pool_w, pool_scale, sconv_w)


OUT_K_SPLIT = 2


def _out_proj_body(x_ref, mod_ref, attn_ref, conv_ref, pool_ref, s_ref, cg_ref, cb_ref, w_ref, lng_ref, lnb_ref,
                   o_ref, a_ref, z_ref, *, alpha):
    k = pl.program_id(1)
    kc = w_ref.shape[0]

    @pl.when(k == OUT_K_SPLIT)
    def _():
        n = conv_ref.shape[1]

        def chunk(rows):
            mean, rstd = _row_moments(lambda cols: conv_ref[rows, cols], n)
            for cols in _col_chunks(n):
                y = (conv_ref[rows, cols] - mean) * rstd * cg_ref[:, cols] + cb_ref[:, cols]
                a_ref[rows, cols] = _silu(y).astype(a_ref.dtype)
        _row_chunks(conv_ref.shape[0], chunk)

    for piece, src_ref in enumerate((attn_ref, a_ref, pool_ref, s_ref)):
        for part in range(OUT_K_SPLIT):
            step = piece * OUT_K_SPLIT + part

            @pl.when(k == step)
            def _(src_ref=src_ref, part=part, step=step):
                contrib = jnp.dot(src_ref[:, pl.ds(part * kc, kc)], w_ref[...], preferred_element_type=F32)
                if step == 0:
                    o_ref[...] = contrib
                else:
                    o_ref[...] += contrib

    @pl.when(k == pl.num_programs(1) - 1)
    def _():
        _residual_norm_into(x_ref, mod_ref, 1, 1.0, alpha, lng_ref, lnb_ref, o_ref, o_ref, z_ref)


def _out_proj(x, mod, attn, conv, pool, s, conv_ln_g, conv_ln_b, w_out, ln_g, ln_b, *, layer, alpha, seq, tm=512):
    m, d = x.shape
    width = attn.shape[1]
    kc = width // OUT_K_SPLIT
    assert m % tm == 0 and w_out.shape[1] == 4 * width and (seq % tm == 0 or mod.shape[0] == 1)
    piece = pl.BlockSpec((tm, width), lambda i, k: (i, 0))
    body = functools.partial(_out_proj_body, alpha=alpha)
    return pl.pallas_call(
        body,
        grid=(m // tm, 4 * OUT_K_SPLIT),
        in_specs=[
            pl.BlockSpec((tm, d), lambda i, k: (i, 0)),
            pl.BlockSpec((None, N_MOD, d), _mod_index(mod.shape[0], max(seq // tm, 1))),
            piece, piece, piece, piece,
            pl.BlockSpec((None, 1, width), lambda i, k: (layer, 0, 0)),
            pl.BlockSpec((None, 1, width), lambda i, k: (layer, 0, 0)),
            pl.BlockSpec((None, kc, d), lambda i, k: (layer, k, 0)),
            pl.BlockSpec((None, 1, d), lambda i, k: (layer * 3 + 1, 0, 0)),
            pl.BlockSpec((None, 1, d), lambda i, k: (layer * 3 + 1, 0, 0)),
        ],
        out_specs=pl.BlockSpec((tm, d), lambda i, k: (i, 0)),
        out_shape=jax.ShapeDtypeStruct((m, d), F32),
        scratch_shapes=[pltpu.VMEM((tm, width), BF16), pltpu.VMEM((LN_UNROLL * ROW_CHUNK, d), F32)],
        compiler_params=_params(
            ("parallel", "arbitrary"),
            _vmem_limit([((tm, d), F32), ((tm, d), F32), ((kc, d), BF16), ((tm, width), F32),
                         ((tm, width), BF16), ((tm, width), BF16), ((tm, width), BF16), ((N_MOD, d), F32)],
                        [((tm, width), BF16), ((tm, d), F32), ((LN_UNROLL * ROW_CHUNK, d), F32)])),
        name="out_proj",
    )(x, mod, attn, conv, pool, s, conv_ln_g, conv_ln_b, w_out, ln_g, ln_b)


def kernel(x_prompt, x_sample, cache_k, cache_v, c, c_ctx, w_mod, b_mod, ln_g, ln_b, ffn_w_gu, ffn_w_down, w_in,
           q_norm_g, k_norm_g, conv_dw_w, conv_dw_b, conv_ln_g, conv_ln_b, pool_w, pool_scale, sconv_w, w_out):
    depth = w_mod.shape[0]
    d = x_prompt.shape[-1]
    alpha = float((2 * depth) ** 0.25)
    n_dec = c.shape[0]

    w_in_b = w_in.astype(BF16)
    w_out_b = w_out.astype(BF16)

    n_rows = 1 + n_dec
    padded = -(-n_rows // V7X_SUBLANES) * V7X_SUBLANES
    cvec = jnp.concatenate([c_ctx[None, :], c, jnp.zeros((padded - n_rows, d), F32)], axis=0)
    mod_all = _modulation(cvec, w_mod, b_mod).reshape(depth, padded, N_MOD, d)

    ln_g3 = ln_g.reshape(depth * 3, 1, d)
    ln_b3 = ln_b.reshape(depth * 3, 1, d)
    vec3 = lambda a: a.reshape(depth, 1, a.shape[-1])
    gq, gk = vec3(q_norm_g), vec3(k_norm_g)
    cb, cg, cbias, ps = vec3(conv_ln_b), vec3(conv_ln_g), vec3(conv_dw_b), vec3(pool_scale)

    def run_group(x3, mods, ctx_k, ctx_v, emit_state):
        batch, seq, _ = x3.shape
        x = x3.reshape(batch * seq, d)
        states = []
        for l in range(depth):
            mod = mods[l]
            common = dict(layer=l, seq=seq)
            x = _ffn(x, mod, ffn_w_gu, ffn_w_down, ln_g3, ln_b3, ffn_idx=0, sub=0, alpha=alpha, **common)
            h = _in_proj(x, mod, w_in_b, **common)
            attn = _attention(h, gq, gk, layer=l, batch=batch, seq=seq, ctx_k=ctx_k, ctx_v=ctx_v,
                              emit_state=emit_state)
            if emit_state:
                attn, k_l, v_l = attn
                states.append((k_l, v_l))
            else:
                attn, = attn
            conv, pool, s = _local_mix(h, conv_dw_w, cbias, pool_w, ps, sconv_w, layer=l, batch=batch, seq=seq)
            x = _out_proj(x, mod, attn, conv, pool, s, cg, cb, w_out_b, ln_g3, ln_b3, alpha=alpha, **common)
            x = _ffn(x, mod, ffn_w_gu, ffn_w_down, ln_g3, ln_b3, ffn_idx=1, sub=2, alpha=alpha, **common)
        return x.reshape(batch, seq, d), states

    mods_ctx = [mod_all[l, 0:1] for l in range(depth)]
    mods_dec = [mod_all[l, 1:1 + n_dec] for l in range(depth)]
    y_prompt, states = run_group(x_prompt, mods_ctx, None, None, True)
    y_sample, _ = run_group(x_sample, mods_dec, cache_k, cache_v, False)
    state_k = jnp.stack([k for k, _ in states], axis=1)
    state_v = jnp.stack([v for _, v in states], axis=1)
    return y_prompt, y_sample, state_k, state_v
```
